```python
import math
import jax, jax.numpy as jnp
from jax import lax
import numpy as np

D_MODEL = 1024
BATCH = 8
SEQ = 2048
DEPTH = 2

N_META = 16
D_A = D_MODEL // 2
D_B = D_MODEL // 2
CONV_A_WIDTH = 31
POOL_WINDOWS = (2, 4, 8, 16)
N_POOL = len(POOL_WINDOWS)
POOL_C = D_B // N_POOL
D_IN_AB = 2 * D_A + D_B
D_C = D_MODEL
CONV_C_WIDTH = 3
D_IN_C = 3 * D_C
D_FF = 2816
CONV_F_WIDTH = 3
DEEPNORM_ALPHA = (2.0 * DEPTH) ** 0.25
DEEPNORM_BETA = (8.0 * DEPTH) ** -0.25
LN_EPS = 1e-5
N_EVEN = (DEPTH + 1) // 2
N_ODD = DEPTH // 2

kernel_name = "hybrid_conformer_pool_shortconv_encoder"


def layer_norm(x, g, b):
    x32 = x.astype(jnp.float32)
    mu = jnp.mean(x32, axis=-1, keepdims=True)
    var = jnp.mean(jnp.square(x32 - mu), axis=-1, keepdims=True)
    y = (x32 - mu) * lax.rsqrt(var + LN_EPS) * g.astype(jnp.float32) + b.astype(jnp.float32)
    return y.astype(x.dtype)


def dwconv(x, w, b):
    k = w.shape[0]
    pad = k // 2
    y = lax.conv_general_dilated(
        x, w[:, None, :].astype(x.dtype), window_strides=(1,), padding=[(pad, pad)],
        dimension_numbers=("NWC", "WIO", "NWC"), feature_group_count=x.shape[-1])
    return y + b


def multiscale_pool(u):
    bsz, length = u.shape[0], u.shape[1]
    u32 = u.astype(jnp.float32)
    csum = jnp.concatenate(
        [jnp.zeros((bsz, 1) + u.shape[2:], jnp.float32), jnp.cumsum(u32, axis=1)], axis=1)
    t = jnp.arange(length)
    outs = []
    for g, w in enumerate(POOL_WINDOWS):
        lo = jnp.clip(t - w // 2, 0, length)
        hi = jnp.clip(t + (w - w // 2), 0, length)
        s = jnp.take(csum[:, :, g], hi, axis=1) - jnp.take(csum[:, :, g], lo, axis=1)
        cnt = (hi - lo).astype(jnp.float32)
        outs.append(s / cnt[None, :, None] - u32[:, :, g])
    return jnp.stack(outs, axis=2).astype(u.dtype)


def mixer_ab(x, w_in, b_in, conv_w, conv_b, n_g, n_b, pool_w, pool_scale, w_out, b_out):
    bsz, length, _ = x.shape
    h = x @ w_in + b_in
    a_val, a_gate, u_b = jnp.split(h, [D_A, 2 * D_A], axis=-1)
    a = a_val * jax.nn.sigmoid(a_gate)
    a = dwconv(a, conv_w, conv_b)
    a = jax.nn.silu(layer_norm(a, n_g, n_b))
    p = multiscale_pool(u_b.reshape(bsz, length, N_POOL, POOL_C))
    p = jnp.einsum("blgc,gcd->blgd", p, pool_w).reshape(bsz, length, D_B) * pool_scale
    return jnp.concatenate([a, p], axis=-1) @ w_out + b_out


def mixer_c(x, w_in, b_in, conv_w, conv_b, w_out, b_out):
    h = x @ w_in + b_in
    bg, cg, v = jnp.split(h, 3, axis=-1)
    y = bg * dwconv(cg * v, conv_w, conv_b)
    return y @ w_out + b_out


def conv_glu(x, w_up, b_up, conv_w, conv_b, w_down, b_down):
    h = x @ w_up + b_up
    g, v = jnp.split(h, 2, axis=-1)
    g = dwconv(g, conv_w, conv_b)
    return (jax.nn.silu(g) * v) @ w_down + b_down


def setup_inputs(seed: int = 0) -> dict:
    key = jax.random.key(seed)
    ks = iter(jax.random.split(key, 40))
    nrm = lambda shape, s: jax.random.normal(next(ks), shape, jnp.float32) * s
    d = D_MODEL
    return {
        "x": nrm((BATCH, SEQ, d), 1.0),
        "meta_tokens": nrm((N_META, d), 1.0),
        "w_in_ab": nrm((N_EVEN, d, D_IN_AB), d ** -0.5),
        "b_in_ab": nrm((N_EVEN, D_IN_AB), 0.02),
        "conv_a_w": nrm((N_EVEN, CONV_A_WIDTH, D_A), CONV_A_WIDTH ** -0.5),
        "conv_a_b": nrm((N_EVEN, D_A), 0.02),
        "norm_a_g": 1.0 + nrm((N_EVEN, D_A), 0.05),
        "norm_a_b": nrm((N_EVEN, D_A), 0.02),
        "pool_w": nrm((N_EVEN, N_POOL, POOL_C, POOL_C), POOL_C ** -0.5),
        "pool_scale": 1.0 + nrm((N_EVEN, D_B), 0.1),
        "w_out_ab": nrm((N_EVEN, D_A + D_B, d), (D_A + D_B) ** -0.5 * DEEPNORM_BETA),
        "b_out_ab": nrm((N_EVEN, d), 0.02),
        "w_in_c": nrm((N_ODD, d, D_IN_C), d ** -0.5),
        "b_in_c": nrm((N_ODD, D_IN_C), 0.02),
        "conv_c_w": nrm((N_ODD, CONV_C_WIDTH, D_C), CONV_C_WIDTH ** -0.5),
        "conv_c_b": nrm((N_ODD, D_C), 0.02),
        "w_out_c": nrm((N_ODD, D_C, d), D_C ** -0.5 * DEEPNORM_BETA),
        "b_out_c": nrm((N_ODD, d), 0.02),
        "mix_ln_g": 1.0 + nrm((DEPTH, d), 0.05),
        "mix_ln_b": nrm((DEPTH, d), 0.02),
        "ffn_w_up": nrm((DEPTH, d, 2 * D_FF), d ** -0.5),
        "ffn_b_up": nrm((DEPTH, 2 * D_FF), 0.02),
        "ffn_conv_w": nrm((DEPTH, CONV_F_WIDTH, D_FF), CONV_F_WIDTH ** -0.5),
        "ffn_conv_b": nrm((DEPTH, D_FF), 0.02),
        "ffn_w_down": nrm((DEPTH, D_FF, d), D_FF ** -0.5 * DEEPNORM_BETA),
        "ffn_b_down": nrm((DEPTH, d), 0.02),
        "ffn_ln_g": 1.0 + nrm((DEPTH, d), 0.05),
        "ffn_ln_b": nrm((DEPTH, d), 0.02),
    }


def reference(x, meta_tokens, w_in_ab, b_in_ab, conv_a_w, conv_a_b, norm_a_g, norm_a_b,
              pool_w, pool_scale, w_out_ab, b_out_ab, w_in_c, b_in_c, conv_c_w, conv_c_b,
              w_out_c, b_out_c, mix_ln_g, mix_ln_b, ffn_w_up, ffn_b_up, ffn_conv_w,
              ffn_conv_b, ffn_w_down, ffn_b_down, ffn_ln_g, ffn_ln_b):
    bsz = x.shape[0]
    meta = jnp.broadcast_to(meta_tokens[None].astype(x.dtype), (bsz, N_META, x.shape[-1]))
    h = jnp.concatenate([meta, x], axis=1)
    for i in range(DEPTH):
        if i % 2 == 0:
            j = i // 2
            m = mixer_ab(h, w_in_ab[j], b_in_ab[j], conv_a_w[j], conv_a_b[j], norm_a_g[j],
                         norm_a_b[j], pool_w[j], pool_scale[j], w_out_ab[j], b_out_ab[j])
        else:
            j = i // 2
            m = mixer_c(h, w_in_c[j], b_in_c[j], conv_c_w[j], conv_c_b[j], w_out_c[j], b_out_c[j])
        h = layer_norm(DEEPNORM_ALPHA * h + m, mix_ln_g[i], mix_ln_b[i])
        f = conv_glu(h, ffn_w_up[i], ffn_b_up[i], ffn_conv_w[i], ffn_conv_b[i],
                     ffn_w_down[i], ffn_b_down[i])
        h = layer_norm(DEEPNORM_ALPHA * h + f, ffn_ln_g[i], ffn_ln_b[i])
    return h[:, N_META:]
```

```python
import functools

import jax
import jax.numpy as jnp
from jax import lax
from jax.experimental import pallas as pl
from jax.experimental.pallas import tpu as pltpu

N_META = 16
POOL_WINDOWS = (2, 4, 8, 16)
LN_EPS = 1e-5
DEPTH = 2
DEEPNORM_ALPHA = (2.0 * DEPTH) ** 0.25

SEQ_TILE = 688
HALO_AB = 16
HALO_SMALL = 8
CHUNK = 256
LN_ROWS = 16
CONV_ROWS = 16
VMEM_LIMIT_BYTES = 58 * 1024 * 1024

_BF16 = jnp.bfloat16
_F32 = jnp.float32


def _dot(a, b):
    return jnp.dot(a, b, preferred_element_type=_F32)


def _layer_norm(x, g, b):
    mu = jnp.mean(x, axis=-1, keepdims=True)
    xc = x - mu
    var = jnp.mean(xc * xc, axis=-1, keepdims=True)
    return xc * lax.rsqrt(var + LN_EPS) * g + b


def _assemble(xe_ref, prev_ref, x_ref, next_ref, halo):
    tl = x_ref.shape[1]
    xe_ref[0:halo, :] = prev_ref[0]
    xe_ref[halo:halo + tl, :] = x_ref[0]
    xe_ref[halo + tl:halo + tl + halo, :] = next_ref[0]


def _valid_rows(n_rows, tile_start, halo, seq_len):
    pos = lax.broadcasted_iota(jnp.int32, (n_rows, 1), 0) + (tile_start - halo)
    return (pos >= 0) & (pos < seq_len)


def _residual_ln_epilogue(x_ref, o_ref, bias_ref, g_ref, b_ref):
    tl = x_ref.shape[1]
    bias = bias_ref[...]
    g = g_ref[...]
    b = b_ref[...]

    def body(r, carry):
        rows = pl.ds(pl.multiple_of(r * LN_ROWS, LN_ROWS), LN_ROWS)
        z = DEEPNORM_ALPHA * x_ref[0, rows, :] + (o_ref[0, rows, :] + bias)
        o_ref[0, rows, :] = _layer_norm(z, g, b)
        return carry

    lax.fori_loop(0, tl // LN_ROWS, body, 0)


def _ffn_kernel(prev_ref, x_ref, next_ref, wup_ref, bup_ref, cw_ref, cb_ref, wdn_ref,
                bdn_ref, lng_ref, lnb_ref, o_ref, xe_ref, g_ref, a_ref, *, seq_len):
    tl = x_ref.shape[1]
    h = HALO_SMALL
    d_ff = wdn_ref.shape[0]
    i = pl.program_id(1)

    _assemble(xe_ref, prev_ref, x_ref, next_ref, h)
    xb = xe_ref[...].astype(_BF16)
    valid = _valid_rows(tl + 2 * h, i * tl, h, seq_len)

    for j in range(d_ff // CHUNK):
        c0 = j * CHUNK
        gsl = slice(c0, c0 + CHUNK)
        vsl = slice(d_ff + c0, d_ff + c0 + CHUNK)
        g = _dot(xb, wup_ref[:, gsl]) + bup_ref[:, gsl]
        gbuf = g_ref.at[j % 2]
        gbuf[...] = jnp.where(valid, g, 0.0)
        v = _dot(xb, wup_ref[:, vsl]) + bup_ref[:, vsl]
        gc = (cw_ref[0:1, gsl] * gbuf[h - 1:h - 1 + tl, :]
              + cw_ref[1:2, gsl] * gbuf[h:h + tl, :]
              + cw_ref[2:3, gsl] * gbuf[h + 1:h + 1 + tl, :]
              + cb_ref[:, gsl])
        act = jax.nn.silu(gc) * v[h:h + tl, :]
        a_ref[:, gsl] = act.astype(_BF16)

    o_ref[0] = _dot(a_ref[...], wdn_ref[...])
    _residual_ln_epilogue(x_ref, o_ref, bdn_ref, lng_ref, lnb_ref)


def _mixer_c_kernel(prev_ref, x_ref, next_ref, win_ref, bin_ref, cw_ref, cb_ref, wout_ref,
                    bout_ref, lng_ref, lnb_ref, o_ref, xe_ref, s_ref, y_ref, *, seq_len):
    tl = x_ref.shape[1]
    h = HALO_SMALL
    d_c = wout_ref.shape[0]
    i = pl.program_id(1)

    _assemble(xe_ref, prev_ref, x_ref, next_ref, h)
    xb = xe_ref[...].astype(_BF16)
    valid = _valid_rows(tl + 2 * h, i * tl, h, seq_len)

    for j in range(d_c // CHUNK):
        c0 = j * CHUNK
        bsl = slice(c0, c0 + CHUNK)
        csl = slice(d_c + c0, d_c + c0 + CHUNK)
        vsl = slice(2 * d_c + c0, 2 * d_c + c0 + CHUNK)
        cg = _dot(xb, win_ref[:, csl]) + bin_ref[:, csl]
        v = _dot(xb, win_ref[:, vsl]) + bin_ref[:, vsl]
        sbuf = s_ref.at[j % 2]
        sbuf[...] = jnp.where(valid, cg * v, 0.0)
        bg = _dot(xb, win_ref[:, bsl]) + bin_ref[:, bsl]
        conv = (cw_ref[0:1, bsl] * sbuf[h - 1:h - 1 + tl, :]
                + cw_ref[1:2, bsl] * sbuf[h:h + tl, :]
                + cw_ref[2:3, bsl] * sbuf[h + 1:h + 1 + tl, :]
                + cb_ref[:, bsl])
        y_ref[:, bsl] = (bg[h:h + tl, :] * conv).astype(_BF16)

    o_ref[0] = _dot(y_ref[...], wout_ref[...])
    _residual_ln_epilogue(x_ref, o_ref, bout_ref, lng_ref, lnb_ref)


def _mixer_ab_kernel(prev_ref, x_ref, next_ref, win_ref, bin_ref, cw_ref, cb_ref, ng_ref,
                     nb_ref, pw_ref, ps_ref, wout_ref, bout_ref, lng_ref, lnb_ref, o_ref,
                     xe_ref, a_ref, u_ref, c_ref, cat_ref, *, seq_len):
    tl = x_ref.shape[1]
    h = HALO_AB
    d_a = a_ref.shape[1]
    d_b = u_ref.shape[1]
    pool_c = d_b // len(POOL_WINDOWS)
    k_taps = cw_ref.shape[0]
    reach = k_taps // 2
    i = pl.program_id(1)

    _assemble(xe_ref, prev_ref, x_ref, next_ref, h)
    xb = xe_ref[...].astype(_BF16)
    valid = _valid_rows(tl + 2 * h, i * tl, h, seq_len)

    for c0 in range(0, d_a, CHUNK):
        vsl = slice(c0, c0 + CHUNK)
        gsl = slice(d_a + c0, d_a + c0 + CHUNK)
        a_val = _dot(xb, win_ref[:, vsl]) + bin_ref[:, vsl]
        a_gate = _dot(xb, win_ref[:, gsl]) + bin_ref[:, gsl]
        a_ref[:, vsl] = jnp.where(valid, a_val * jax.nn.sigmoid(a_gate), 0.0)
    for c0 in range(0, d_b, CHUNK):
        usl = slice(2 * d_a + c0, 2 * d_a + c0 + CHUNK)
        u = _dot(xb, win_ref[:, usl]) + bin_ref[:, usl]
        u_ref[:, c0:c0 + CHUNK] = jnp.where(valid, u, 0.0)

    for r0 in range(0, tl, CONV_ROWS):
        acc = cw_ref[0:1, :] * a_ref[r0 + h - reach:r0 + h - reach + CONV_ROWS, :]
        for k in range(1, k_taps):
            s = r0 + h - reach + k
            acc = acc + cw_ref[k:k + 1, :] * a_ref[s:s + CONV_ROWS, :]
        c_ref[r0:r0 + CONV_ROWS, :] = acc + cb_ref[...]

    ng = ng_ref[...]
    nb = nb_ref[...]

    def norm_body(r, carry):
        rows = pl.ds(pl.multiple_of(r * LN_ROWS, LN_ROWS), LN_ROWS)
        cat_ref[rows, 0:d_a] = jax.nn.silu(_layer_norm(c_ref[rows, :], ng, nb)).astype(_BF16)
        return carry

    lax.fori_loop(0, tl // LN_ROWS, norm_body, 0)

    pos = lax.broadcasted_iota(jnp.int32, (tl, 1), 0) + i * tl
    for g, w in enumerate(POOL_WINDOWS):
        half = w // 2
        lanes = slice(g * pool_c, (g + 1) * pool_c)
        s = u_ref[h - half:h - half + tl, lanes]
        for d in range(-half + 1, half):
            s = s + u_ref[h + d:h + d + tl, lanes]
        cnt = (jnp.minimum(pos + half, seq_len) - jnp.maximum(pos - half, 0)).astype(_F32)
        p = s / cnt - u_ref[h:h + tl, lanes]
        q = _dot(p.astype(_BF16), pw_ref[g]) * ps_ref[:, lanes]
        cat_ref[:, d_a + g * pool_c:d_a + (g + 1) * pool_c] = q.astype(_BF16)

    o_ref[0] = _dot(cat_ref[...], wout_ref[...])
    _residual_ln_epilogue(x_ref, o_ref, bout_ref, lng_ref, lnb_ref)


def _tile_specs(seq_len, d_model, halo):
    per_tile = SEQ_TILE // halo
    last = seq_len // halo - 1
    prev = pl.BlockSpec((1, halo, d_model),
                        lambda b, i: (b, jnp.maximum(i * per_tile - 1, 0), 0))
    main = pl.BlockSpec((1, SEQ_TILE, d_model), lambda b, i: (b, i, 0))
    nxt = pl.BlockSpec((1, halo, d_model),
                       lambda b, i: (b, jnp.minimum((i + 1) * per_tile, last), 0))
    return [prev, main, nxt]


def _resident(shape):
    zeros = (0,) * len(shape)
    return pl.BlockSpec(shape, lambda b, i: zeros, pipeline_mode=pl.Buffered(1))


def _call(kernel, h, params, scratch, halo, name):
    bsz, seq_len, d_model = h.shape
    assert seq_len % SEQ_TILE == 0 and SEQ_TILE % halo == 0 and SEQ_TILE % LN_ROWS == 0
    return pl.pallas_call(
        functools.partial(kernel, seq_len=seq_len),
        grid=(bsz, seq_len // SEQ_TILE),
        in_specs=_tile_specs(seq_len, d_model, halo) + [_resident(p.shape) for p in params],
        out_specs=pl.BlockSpec((1, SEQ_TILE, d_model), lambda b, i: (b, i, 0)),
        out_shape=jax.ShapeDtypeStruct(h.shape, _F32),
        scratch_shapes=scratch,
        compiler_params=pltpu.CompilerParams(
            dimension_semantics=("arbitrary", "arbitrary"),
            vmem_limit_bytes=VMEM_LIMIT_BYTES),
        name=name,
    )(h, h, h, *params)


def _row(v):
    return v.reshape(1, -1)


def _ffn(h, w_up, b_up, conv_w, conv_b, w_down, b_down, ln_g, ln_b, name):
    d_model = h.shape[-1]
    d_ff = w_down.shape[0]
    assert d_ff % CHUNK == 0
    ext = SEQ_TILE + 2 * HALO_SMALL
    params = [w_up.astype(_BF16), _row(b_up), conv_w, _row(conv_b), w_down.astype(_BF16),
              _row(b_down), _row(ln_g), _row(ln_b)]
    scratch = [pltpu.VMEM((ext, d_model), _F32),
               pltpu.VMEM((2, ext, CHUNK), _F32),
               pltpu.VMEM((SEQ_TILE, d_ff), _BF16)]
    return _call(_ffn_kernel, h, params, scratch, HALO_SMALL, name)


def _mixer_c(h, w_in, b_in, conv_w, conv_b, w_out, b_out, ln_g, ln_b):
    d_model = h.shape[-1]
    d_c = w_out.shape[0]
    assert d_c % CHUNK == 0
    ext = SEQ_TILE + 2 * HALO_SMALL
    params = [w_in.astype(_BF16), _row(b_in), conv_w, _row(conv_b), w_out.astype(_BF16),
              _row(b_out), _row(ln_g), _row(ln_b)]
    scratch = [pltpu.VMEM((ext, d_model), _F32),
               pltpu.VMEM((2, ext, CHUNK), _F32),
               pltpu.VMEM((SEQ_TILE, d_c), _BF16)]
    return _call(_mixer_c_kernel, h, params, scratch, HALO_SMALL, "mixer_c")


def _mixer_ab(h, w_in, b_in, conv_w, conv_b, n_g, n_b, pool_w, pool_scale, w_out, b_out,
              ln_g, ln_b):
    d_model = h.shape[-1]
    d_a = conv_w.shape[1]
    d_b = pool_scale.shape[0]
    assert conv_w.shape[0] // 2 <= HALO_AB and max(POOL_WINDOWS) // 2 <= HALO_AB
    assert d_a % CHUNK == 0 and d_b % CHUNK == 0 and SEQ_TILE % CONV_ROWS == 0
    ext = SEQ_TILE + 2 * HALO_AB
    params = [w_in.astype(_BF16), _row(b_in), conv_w, _row(conv_b), _row(n_g), _row(n_b),
              pool_w.astype(_BF16), _row(pool_scale), w_out.astype(_BF16), _row(b_out),
              _row(ln_g), _row(ln_b)]
    scratch = [pltpu.VMEM((ext, d_model), _F32),
               pltpu.VMEM((ext, d_a), _F32),
               pltpu.VMEM((ext, d_b), _F32),
               pltpu.VMEM((SEQ_TILE, d_a), _F32),
               pltpu.VMEM((SEQ_TILE, d_a + d_b), _BF16)]
    return _call(_mixer_ab_kernel, h, params, scratch, HALO_AB, "mixer_ab")


def kernel(x, meta_tokens, w_in_ab, b_in_ab, conv_a_w, conv_a_b, norm_a_g, norm_a_b, pool_w, pool_scale, w_out_ab, b_out_ab, w_in_c, b_in_c, conv_c_w, conv_c_b, w_out_c, b_out_c, mix_ln_g, mix_ln_b, ffn_w_up, ffn_b_up, ffn_conv_w, ffn_conv_b, ffn_w_down, ffn_b_down, ffn_ln_g, ffn_ln_b):
    bsz, _, d_model = x.shape
    assert meta_tokens.shape[0] == N_META and mix_ln_g.shape[0] == DEPTH
    meta = jnp.broadcast_to(meta_tokens[None].astype(x.dtype), (bsz, N_META, d_model))
    h = jnp.concatenate([meta, x], axis=1)
    for i in range(DEPTH):
        j = i // 2
        if i % 2 == 0:
            h = _mixer_ab(h, w_in_ab[j], b_in_ab[j], conv_a_w[j], conv_a_b[j], norm_a_g[j],
                          norm_a_b[j], pool_w[j], pool_scale[j], w_out_ab[j], b_out_ab[j],
                          mix_ln_g[i], mix_ln_b[i])
        else:
            h = _mixer_c(h, w_in_c[j], b_in_c[j], conv_c_w[j], conv_c_b[j], w_out_c[j],
                         b_out_c[j], mix_ln_g[i], mix_ln_b[i])
        h = _ffn(h, ffn_w_up[i], ffn_b_up[i], ffn_conv_w[i], ffn_conv_b[i], ffn_w_down[i],
                 ffn_b_down[i], ffn_ln_g[i], ffn_ln_b[i], f"conv_glu_{i}")
    return h[:, N_META:]
```

```python
import functools

import jax
import jax.numpy as jnp
from jax import lax
from jax.experimental import pallas as pl
from jax.experimental.pallas import tpu as pltpu

N_META = 16
POOL_WINDOWS = (2, 4, 8, 16)
LN_EPS = 1e-5
DEPTH = 2
DEEPNORM_ALPHA = (2.0 * DEPTH) ** 0.25

SUBLANES = 8
SEQ_TILE = 512
PREV_ROWS = 32
HALO_AB = 16
HALO_SMALL = 8
CHUNK = 256
LN_ROWS = 16
CONV_ROWS = 48
VMEM_LIMIT_BYTES = 58 * 1024 * 1024

_BF16 = jnp.bfloat16
_F32 = jnp.float32


def _dot(a, b):
    return jnp.dot(a, b, preferred_element_type=_F32)


def _layer_norm(x, g, b):
    mu = jnp.mean(x, axis=-1, keepdims=True)
    xc = x - mu
    var = jnp.mean(xc * xc, axis=-1, keepdims=True)
    return xc * lax.rsqrt(var + LN_EPS) * g + b


def _assemble(xe_ref, prev_ref, x_ref, next_ref, meta_ref, halo):
    tl = x_ref.shape[1]
    first = pl.program_id(1) == 0
    xe_ref[0:halo, :] = prev_ref[0, PREV_ROWS - N_META - halo:PREV_ROWS - N_META, :]
    xe_ref[halo:halo + N_META, :] = jnp.where(first, meta_ref[0],
                                              prev_ref[0, PREV_ROWS - N_META:PREV_ROWS, :])
    xe_ref[halo + N_META:halo + N_META + tl, :] = x_ref[0]
    xe_ref[halo + N_META + tl:halo + N_META + tl + halo, :] = next_ref[0]


def _valid_rows(n_rows, halo, tl, seq_len):
    pos = (lax.broadcasted_iota(jnp.int32, (n_rows, 1), 0)
           + (pl.program_id(1) * tl - N_META - halo))
    return (pos >= -N_META) & (pos < seq_len)


def _row_blocks(n_rows, n_blocks, multiple):
    units = n_rows // multiple
    bounds = [multiple * ((units * k) // n_blocks) for k in range(n_blocks + 1)]
    return list(zip(bounds[:-1], bounds[1:]))


def _project_residual_ln(lhs_ref, w_ref, bias_ref, g_ref, b_ref, xe_ref, halo, mm_ref, y0_ref,
                         o_ref, meta_out_ref):
    m_rows = lhs_ref.shape[0]
    bias = bias_ref[...]
    g = g_ref[...]
    b = b_ref[...]
    for r0, r1 in _row_blocks(m_rows, 2, LN_ROWS):
        mm_ref[r0:r1, :] = _dot(lhs_ref[r0:r1, :], w_ref[...])
        for s0 in range(r0, r1, LN_ROWS):
            z = (DEEPNORM_ALPHA * xe_ref[halo + s0:halo + s0 + LN_ROWS, :]
                 + (mm_ref[s0:s0 + LN_ROWS, :] + bias))
            y = _layer_norm(z, g, b)
            if s0 < N_META:
                y0_ref[s0:s0 + LN_ROWS, :] = y
            else:
                o_ref[0, s0 - N_META:s0 - N_META + LN_ROWS, :] = y
    if meta_out_ref is not None:
        @pl.when(pl.program_id(1) == 0)
        def _():
            meta_out_ref[0] = y0_ref[...]


def _split_outs(refs, emit_meta):
    if emit_meta:
        return refs[0], refs[1], refs[2:]
    return refs[0], None, refs[1:]


def _ffn_kernel(prev_ref, x_ref, next_ref, meta_ref, wup_ref, bup_ref, cw_ref, cb_ref, wdn_ref,
                bdn_ref, lng_ref, lnb_ref, *rest, seq_len, emit_meta):
    o_ref, meta_out_ref, (xe_ref, g_ref, a_ref, mm_ref, y0_ref) = _split_outs(rest, emit_meta)
    tl = x_ref.shape[1]
    h = HALO_SMALL
    m = tl + N_META
    d_ff = wdn_ref.shape[0]

    _assemble(xe_ref, prev_ref, x_ref, next_ref, meta_ref, h)
    xb = xe_ref[...].astype(_BF16)
    xbm = xe_ref[h:h + m, :].astype(_BF16)
    valid = _valid_rows(m + 2 * h, h, tl, seq_len)

    for j in range(d_ff // CHUNK):
        c0 = j * CHUNK
        gsl = slice(c0, c0 + CHUNK)
        vsl = slice(d_ff + c0, d_ff + c0 + CHUNK)
        g = _dot(xb, wup_ref[:, gsl]) + bup_ref[:, gsl]
        gbuf = g_ref.at[j % 2]
        gbuf[...] = jnp.where(valid, g, 0.0)
        v = _dot(xbm, wup_ref[:, vsl]) + bup_ref[:, vsl]
        gc = (cw_ref[1:2, gsl] * gbuf[h:h + m, :]
              + cw_ref[0:1, gsl] * gbuf[h - 1:h - 1 + m, :]
              + cw_ref[2:3, gsl] * gbuf[h + 1:h + 1 + m, :]
              + cb_ref[:, gsl])
        a_ref[:, gsl] = (jax.nn.silu(gc) * v).astype(_BF16)

    _project_residual_ln(a_ref, wdn_ref, bdn_ref, lng_ref, lnb_ref, xe_ref, h, mm_ref, y0_ref,
                         o_ref, meta_out_ref)


def _mixer_c_kernel(prev_ref, x_ref, next_ref, meta_ref, win_ref, bin_ref, cw_ref, cb_ref,
                    wout_ref, bout_ref, lng_ref, lnb_ref, *rest, seq_len, emit_meta):
    o_ref, meta_out_ref, (xe_ref, s_ref, y_ref, mm_ref, y0_ref) = _split_outs(rest, emit_meta)
    tl = x_ref.shape[1]
    h = HALO_SMALL
    m = tl + N_META
    d_c = wout_ref.shape[0]

    _assemble(xe_ref, prev_ref, x_ref, next_ref, meta_ref, h)
    xb = xe_ref[...].astype(_BF16)
    xbm = xe_ref[h:h + m, :].astype(_BF16)
    valid = _valid_rows(m + 2 * h, h, tl, seq_len)

    for j in range(d_c // CHUNK):
        c0 = j * CHUNK
        bsl = slice(c0, c0 + CHUNK)
        csl = slice(d_c + c0, d_c + c0 + CHUNK)
        vsl = slice(2 * d_c + c0, 2 * d_c + c0 + CHUNK)
        cg = _dot(xb, win_ref[:, csl]) + bin_ref[:, csl]
        v = _dot(xb, win_ref[:, vsl]) + bin_ref[:, vsl]
        sbuf = s_ref.at[j % 2]
        sbuf[...] = jnp.where(valid, cg * v, 0.0)
        bg = _dot(xbm, win_ref[:, bsl]) + bin_ref[:, bsl]
        conv = (cw_ref[1:2, bsl] * sbuf[h:h + m, :]
                + cw_ref[0:1, bsl] * sbuf[h - 1:h - 1 + m, :]
                + cw_ref[2:3, bsl] * sbuf[h + 1:h + 1 + m, :]
                + cb_ref[:, bsl])
        y_ref[:, bsl] = (bg * conv).astype(_BF16)

    _project_residual_ln(y_ref, wout_ref, bout_ref, lng_ref, lnb_ref, xe_ref, h, mm_ref, y0_ref,
                         o_ref, meta_out_ref)


def _mixer_ab_kernel(prev_ref, x_ref, next_ref, meta_ref, win_ref, bin_ref, cwb_ref, cb_ref,
                     ng_ref, nb_ref, pw_ref, ps_ref, wout_ref, bout_ref, lng_ref, lnb_ref,
                     *rest, seq_len, emit_meta):
    o_ref, meta_out_ref, scratch = _split_outs(rest, emit_meta)
    xe_ref, a_ref, ash_ref, u_ref, c_ref, cat_ref, mm_ref, y0_ref = scratch
    tl = x_ref.shape[1]
    h = HALO_AB
    m = tl + N_META
    ext = m + 2 * h
    d_a = a_ref.shape[1]
    d_b = u_ref.shape[1]
    pool_c = d_b // len(POOL_WINDOWS)
    k_taps = cwb_ref.shape[0]
    reach = k_taps // 2

    _assemble(xe_ref, prev_ref, x_ref, next_ref, meta_ref, h)
    xb = xe_ref[...].astype(_BF16)
    valid = _valid_rows(ext, h, tl, seq_len)

    for c0 in range(0, d_a, CHUNK):
        vsl = slice(c0, c0 + CHUNK)
        gsl = slice(d_a + c0, d_a + c0 + CHUNK)
        a_val = _dot(xb, win_ref[:, vsl]) + bin_ref[:, vsl]
        a_gate = _dot(xb, win_ref[:, gsl]) + bin_ref[:, gsl]
        a_ref[:, vsl] = jnp.where(valid, a_val * jax.nn.sigmoid(a_gate), 0.0)
    for c0 in range(0, d_b, CHUNK):
        usl = slice(2 * d_a + c0, 2 * d_a + c0 + CHUNK)
        u = _dot(xb, win_ref[:, usl]) + bin_ref[:, usl]
        u_ref[:, c0:c0 + CHUNK] = jnp.where(valid, u, 0.0)

    n_sh = ext - SUBLANES
    for s in range(1, SUBLANES):
        ash_ref[s - 1, 0:n_sh, :] = a_ref[s:s + n_sh, :]

    def shifted(s, r0, n):
        if s == 0:
            return a_ref[r0:r0 + n, :]
        return ash_ref[s - 1, r0:r0 + n, :]

    for r0 in range(0, m, CONV_ROWS):
        accs = [None] * (CONV_ROWS // SUBLANES)
        for k in range(k_taps):
            first_row = h - reach + k
            s, q = first_row % SUBLANES, first_row // SUBLANES
            w = cwb_ref[k]
            for t in range(len(accs)):
                term = w * shifted(s, r0 + SUBLANES * (q + t), SUBLANES)
                accs[t] = term if accs[t] is None else accs[t] + term
        for t, acc in enumerate(accs):
            c_ref[r0 + SUBLANES * t:r0 + SUBLANES * (t + 1), :] = acc + cb_ref[...]

    ng = ng_ref[...]
    nb = nb_ref[...]
    for r0 in range(0, m, LN_ROWS):
        an = jax.nn.silu(_layer_norm(c_ref[r0:r0 + LN_ROWS, :], ng, nb))
        cat_ref[r0:r0 + LN_ROWS, 0:d_a] = an.astype(_BF16)

    full_len = seq_len + N_META
    tpos = lax.broadcasted_iota(jnp.int32, (m, 1), 0) + pl.program_id(1) * tl
    for g, w in enumerate(POOL_WINDOWS):
        half = w // 2
        lanes = slice(g * pool_c, (g + 1) * pool_c)
        s = u_ref[h:h + m, lanes]
        for d in range(-half, half):
            if d != 0:
                s = s + u_ref[h + d:h + d + m, lanes]
        cnt = (jnp.minimum(tpos + half, full_len) - jnp.maximum(tpos - half, 0)).astype(_F32)
        p = s / cnt - u_ref[h:h + m, lanes]
        q = _dot(p.astype(_BF16), pw_ref[g]) * ps_ref[:, lanes]
        cat_ref[:, d_a + g * pool_c:d_a + (g + 1) * pool_c] = q.astype(_BF16)

    _project_residual_ln(cat_ref, wout_ref, bout_ref, lng_ref, lnb_ref, xe_ref, h, mm_ref,
                         y0_ref, o_ref, meta_out_ref)


def _resident(shape):
    zeros = (0,) * len(shape)
    return pl.BlockSpec(shape, lambda b, i: zeros, pipeline_mode=pl.Buffered(1))


def _call(kernel, x, meta, params, scratch, halo, emit_meta, name):
    bsz, seq_len, d_model = x.shape
    assert seq_len % SEQ_TILE == 0 and SEQ_TILE % PREV_ROWS == 0 and SEQ_TILE % halo == 0
    assert N_META + halo <= PREV_ROWS and (SEQ_TILE + N_META) % LN_ROWS == 0
    assert meta.shape[1:] == (N_META, d_model) and N_META % LN_ROWS == 0
    prev_per_tile = SEQ_TILE // PREV_ROWS
    halo_per_tile = SEQ_TILE // halo
    last_halo = seq_len // halo - 1
    meta_batched = meta.shape[0] != 1
    tile = pl.BlockSpec((1, SEQ_TILE, d_model), lambda b, i: (b, i, 0))
    meta_block = pl.BlockSpec((1, N_META, d_model), lambda b, i: (b, 0, 0))
    in_specs = [
        pl.BlockSpec((1, PREV_ROWS, d_model),
                     lambda b, i: (b, jnp.maximum(i * prev_per_tile - 1, 0), 0)),
        tile,
        pl.BlockSpec((1, halo, d_model),
                     lambda b, i: (b, jnp.minimum((i + 1) * halo_per_tile, last_halo), 0)),
        meta_block if meta_batched else _resident(meta.shape),
    ] + [_resident(p.shape) for p in params]
    out_specs = [tile]
    out_shape = [jax.ShapeDtypeStruct(x.shape, _F32)]
    if emit_meta:
        out_specs.append(meta_block)
        out_shape.append(jax.ShapeDtypeStruct((bsz, N_META, d_model), _F32))
    m = SEQ_TILE + N_META
    scratch = ([pltpu.VMEM((m + 2 * halo, d_model), _F32)] + scratch
               + [pltpu.VMEM((m, d_model), _F32), pltpu.VMEM((N_META, d_model), _F32)])
    outs = pl.pallas_call(
        functools.partial(kernel, seq_len=seq_len, emit_meta=emit_meta),
        grid=(bsz, seq_len // SEQ_TILE),
        in_specs=in_specs,
        out_specs=out_specs,
        out_shape=out_shape,
        scratch_shapes=scratch,
        compiler_params=pltpu.CompilerParams(
            dimension_semantics=("arbitrary", "arbitrary"),
            vmem_limit_bytes=VMEM_LIMIT_BYTES),
        name=name,
    )(x, x, x, meta, *params)
    return (outs[0], outs[1]) if emit_meta else (outs[0], None)


def _row(v):
    return v.reshape(1, -1)


def _ffn(x, meta, w_up, b_up, conv_w, conv_b, w_down, b_down, ln_g, ln_b, emit_meta, name):
    d_ff = w_down.shape[0]
    assert d_ff % CHUNK == 0
    m = SEQ_TILE + N_META
    params = [w_up.astype(_BF16), _row(b_up), conv_w, _row(conv_b), w_down.astype(_BF16),
              _row(b_down), _row(ln_g), _row(ln_b)]
    scratch = [pltpu.VMEM((2, m + 2 * HALO_SMALL, CHUNK), _F32),
               pltpu.VMEM((m, d_ff), _BF16)]
    return _call(_ffn_kernel, x, meta, params, scratch, HALO_SMALL, emit_meta, name)


def _mixer_c(x, meta, w_in, b_in, conv_w, conv_b, w_out, b_out, ln_g, ln_b):
    d_c = w_out.shape[0]
    assert d_c % CHUNK == 0
    m = SEQ_TILE + N_META
    params = [w_in.astype(_BF16), _row(b_in), conv_w, _row(conv_b), w_out.astype(_BF16),
              _row(b_out), _row(ln_g), _row(ln_b)]
    scratch = [pltpu.VMEM((2, m + 2 * HALO_SMALL, CHUNK), _F32),
               pltpu.VMEM((m, d_c), _BF16)]
    return _call(_mixer_c_kernel, x, meta, params, scratch, HALO_SMALL, True, "mixer_c")


def _mixer_ab(x, meta, w_in, b_in, conv_w, conv_b, n_g, n_b, pool_w, pool_scale, w_out, b_out,
              ln_g, ln_b):
    k_taps, d_a = conv_w.shape
    d_b = pool_scale.shape[0]
    assert k_taps // 2 <= HALO_AB and max(POOL_WINDOWS) // 2 <= HALO_AB
    assert d_a % CHUNK == 0 and d_b % CHUNK == 0
    m = SEQ_TILE + N_META
    assert m % CONV_ROWS == 0
    ext = m + 2 * HALO_AB
    conv_w_rows = jnp.broadcast_to(conv_w[:, None, :], (k_taps, SUBLANES, d_a))
    params = [w_in.astype(_BF16), _row(b_in), conv_w_rows, _row(conv_b), _row(n_g), _row(n_b),
              pool_w.astype(_BF16), _row(pool_scale), w_out.astype(_BF16), _row(b_out),
              _row(ln_g), _row(ln_b)]
    scratch = [pltpu.VMEM((ext, d_a), _F32),
               pltpu.VMEM((SUBLANES - 1, ext, d_a), _F32),
               pltpu.VMEM((ext, d_b), _F32),
               pltpu.VMEM((m, d_a), _F32),
               pltpu.VMEM((m, d_a + d_b), _BF16)]
    return _call(_mixer_ab_kernel, x, meta, params, scratch, HALO_AB, True, "mixer_ab")


def kernel(x, meta_tokens, w_in_ab, b_in_ab, conv_a_w, conv_a_b, norm_a_g, norm_a_b, pool_w, pool_scale, w_out_ab, b_out_ab, w_in_c, b_in_c, conv_c_w, conv_c_b, w_out_c, b_out_c, mix_ln_g, mix_ln_b, ffn_w_up, ffn_b_up, ffn_conv_w, ffn_conv_b, ffn_w_down, ffn_b_down, ffn_ln_g, ffn_ln_b):
    assert meta_tokens.shape[0] == N_META and mix_ln_g.shape[0] == DEPTH
    h = x
    meta = meta_tokens[None].astype(x.dtype)
    for i in range(DEPTH):
        j = i // 2
        if i % 2 == 0:
            h, meta = _mixer_ab(h, meta, w_in_ab[j], b_in_ab[j], conv_a_w[j], conv_a_b[j],
                                norm_a_g[j], norm_a_b[j], pool_w[j], pool_scale[j], w_out_ab[j],
                                b_out_ab[j], mix_ln_g[i], mix_ln_b[i])
        else:
            h, meta = _mixer_c(h, meta, w_in_c[j], b_in_c[j], conv_c_w[j], conv_c_b[j],
                               w_out_c[j], b_out_c[j], mix_ln_g[i], mix_ln_b[i])
        h, meta = _ffn(h, meta, ffn_w_up[i], ffn_b_up[i], ffn_conv_w[i], ffn_conv_b[i],
                       ffn_w_down[i], ffn_b_down[i], ffn_ln_g[i], ffn_ln_b[i],
                       i + 1 < DEPTH, f"conv_glu_{i}")
    return h
```

```python
import functools

import jax
import jax.numpy as jnp
from jax import lax
from jax.experimental import pallas as pl
from jax.experimental.pallas import tpu as pltpu

N_META = 16
POOL_WINDOWS = (2, 4, 8, 16)
LN_EPS = 1e-5
DEPTH = 2
DEEPNORM_ALPHA = (2.0 * DEPTH) ** 0.25

SUBLANES = 8
LANES = 128
SEQ_TILE = 512
PREV_ROWS = 64
NEXT_ROWS = 32
HALO_AB = 16
HALO_SMALL = 8
CHUNK = 256
LN_ROWS = 16
CONV_ROWS = 136
FFN_PARAMS = 8
VMEM_LIMIT_BYTES = 60000 * 1024

_BF16 = jnp.bfloat16
_F32 = jnp.float32


def _dot(a, b):
    return jnp.dot(a, b, preferred_element_type=_F32)


def _layer_norm(x, g, b):
    mu = jnp.mean(x, axis=-1, keepdims=True)
    xc = x - mu
    var = jnp.mean(xc * xc, axis=-1, keepdims=True)
    return xc * lax.rsqrt(var + LN_EPS) * g + b


def _row_blocks(n_rows, n_blocks, multiple):
    units = n_rows // multiple
    bounds = [multiple * ((units * k) // n_blocks) for k in range(n_blocks + 1)]
    return list(zip(bounds[:-1], bounds[1:]))


def _valid_rows(n_rows, first_pos, seq_len):
    pos = lax.broadcasted_iota(jnp.int32, (n_rows, 1), 0) + first_pos
    return (pos >= -N_META) & (pos < seq_len)


def _project_residual_ln(lhs_ref, w_ref, bias_ref, g_ref, b_ref, res_ref, res_row0, store):
    bias = bias_ref[...]
    g = g_ref[...]
    b = b_ref[...]
    for r0, r1 in _row_blocks(lhs_ref.shape[0], 2, LN_ROWS):
        proj = _dot(lhs_ref[r0:r1, :], w_ref[...])
        for s0 in range(r0, r1, LN_ROWS):
            z = (DEEPNORM_ALPHA * res_ref[res_row0 + s0:res_row0 + s0 + LN_ROWS, :]
                 + (proj[s0 - r0:s0 - r0 + LN_ROWS, :] + bias))
            store(s0, _layer_norm(z, g, b))


def _ffn_body(xe_ref, first_pos, seq_len, params, o_ref, y0_ref, g_ref, a_ref):
    wup_ref, bup_ref, cw_ref, cb_ref, wdn_ref, bdn_ref, lng_ref, lnb_ref = params
    h = HALO_SMALL
    m = a_ref.shape[0]
    d_ff = wdn_ref.shape[0]

    xb = xe_ref[...].astype(_BF16)
    xbm = xe_ref[h:h + m, :].astype(_BF16)
    valid = _valid_rows(m + 2 * h, first_pos, seq_len)

    for j in range(d_ff // CHUNK):
        c0 = j * CHUNK
        gsl = slice(c0, c0 + CHUNK)
        vsl = slice(d_ff + c0, d_ff + c0 + CHUNK)
        g = _dot(xb, wup_ref[:, gsl]) + bup_ref[:, gsl]
        gbuf = g_ref.at[j % 2]
        gbuf[...] = jnp.where(valid, g, 0.0)
        v = _dot(xbm, wup_ref[:, vsl]) + bup_ref[:, vsl]
        gc = (cw_ref[1:2, gsl] * gbuf[h:h + m, :]
              + cw_ref[0:1, gsl] * gbuf[h - 1:h - 1 + m, :]
              + cw_ref[2:3, gsl] * gbuf[h + 1:h + 1 + m, :]
              + cb_ref[:, gsl])
        a_ref[:, gsl] = (jax.nn.silu(gc) * v).astype(_BF16)

    def store(s0, y):
        if s0 < N_META:
            y0_ref[s0:s0 + LN_ROWS, :] = y
        else:
            o_ref[0, s0 - N_META:s0 - N_META + LN_ROWS, :] = y

    _project_residual_ln(a_ref, wdn_ref, bdn_ref, lng_ref, lnb_ref, xe_ref, h, store)


def _mixer_c_front(xe_ref, valid, hm, m2, params, scratch):
    win_ref, bin_ref, cw_ref, cb_ref = params
    s_ref, y_ref = scratch
    d_c = y_ref.shape[1]
    xb = xe_ref[...].astype(_BF16)
    xbm = xe_ref[hm:hm + m2, :].astype(_BF16)

    for j in range(d_c // CHUNK):
        c0 = j * CHUNK
        bsl = slice(c0, c0 + CHUNK)
        csl = slice(d_c + c0, d_c + c0 + CHUNK)
        vsl = slice(2 * d_c + c0, 2 * d_c + c0 + CHUNK)
        cg = _dot(xb, win_ref[:, csl]) + bin_ref[:, csl]
        v = _dot(xb, win_ref[:, vsl]) + bin_ref[:, vsl]
        sbuf = s_ref.at[j % 2]
        sbuf[...] = jnp.where(valid, cg * v, 0.0)
        bg = _dot(xbm, win_ref[:, bsl]) + bin_ref[:, bsl]
        conv = (cw_ref[1:2, bsl] * sbuf[hm:hm + m2, :]
                + cw_ref[0:1, bsl] * sbuf[hm - 1:hm - 1 + m2, :]
                + cw_ref[2:3, bsl] * sbuf[hm + 1:hm + 1 + m2, :]
                + cb_ref[:, bsl])
        y_ref[:, bsl] = (bg * conv).astype(_BF16)


def _mixer_c_back(hm, m2, params, scratch, *, tpos0, full_len):
    del hm, m2, params, tpos0, full_len
    return scratch[1]


def _mixer_ab_front(xe_ref, valid, hm, m2, params, scratch):
    del hm, m2
    win_ref, bin_ref = params[:2]
    a_ref, _, u_ref = scratch[:3]
    d_a = a_ref.shape[1]
    d_b = u_ref.shape[1]
    xb = xe_ref[...].astype(_BF16)

    for c0 in range(0, d_a, CHUNK):
        vsl = slice(c0, c0 + CHUNK)
        gsl = slice(d_a + c0, d_a + c0 + CHUNK)
        a_val = _dot(xb, win_ref[:, vsl]) + bin_ref[:, vsl]
        a_gate = _dot(xb, win_ref[:, gsl]) + bin_ref[:, gsl]
        a_ref[:, vsl] = jnp.where(valid, a_val * jax.nn.sigmoid(a_gate), 0.0)
    for c0 in range(0, d_b, CHUNK):
        usl = slice(2 * d_a + c0, 2 * d_a + c0 + CHUNK)
        u = _dot(xb, win_ref[:, usl]) + bin_ref[:, usl]
        u_ref[:, c0:c0 + CHUNK] = jnp.where(valid, u, 0.0)


def _mixer_ab_back(hm, m2, params, scratch, *, tpos0, full_len):
    _, _, cwb_ref, cb_ref, ng_ref, nb_ref, pw_ref, ps_ref = params
    a_ref, ash_ref, u_ref, c_ref, cat_ref = scratch
    ext = m2 + 2 * hm
    d_a = a_ref.shape[1]
    d_b = u_ref.shape[1]
    pool_c = d_b // len(POOL_WINDOWS)
    k_taps = cwb_ref.shape[0]
    reach = k_taps // 2

    n_sh = ext - SUBLANES
    n_acc = CONV_ROWS // SUBLANES
    for lt in range(d_a // LANES):
        lanes = slice(lt * LANES, (lt + 1) * LANES)
        sh = ash_ref.at[lt % 2]
        for s in range(1, SUBLANES):
            sh[s - 1, 0:n_sh, :] = a_ref[s:s + n_sh, lanes]
        for r0 in range(0, m2, CONV_ROWS):
            accs = [None] * n_acc
            for k in range(k_taps):
                first_row = hm - reach + k
                s, q = first_row % SUBLANES, first_row // SUBLANES
                w = cwb_ref[k, :, lanes]
                for t in range(n_acc):
                    row = r0 + SUBLANES * (q + t)
                    blk = (a_ref[row:row + SUBLANES, lanes] if s == 0
                           else sh[s - 1, row:row + SUBLANES, :])
                    accs[t] = w * blk if accs[t] is None else accs[t] + w * blk
            for t, acc in enumerate(accs):
                c_ref[r0 + SUBLANES * t:r0 + SUBLANES * (t + 1), lanes] = acc + cb_ref[:, lanes]

    ng = ng_ref[...]
    nb = nb_ref[...]
    for r0 in range(0, m2, LN_ROWS):
        an = jax.nn.silu(_layer_norm(c_ref[r0:r0 + LN_ROWS, :], ng, nb))
        cat_ref[r0:r0 + LN_ROWS, 0:d_a] = an.astype(_BF16)

    tpos = jnp.clip(lax.broadcasted_iota(jnp.int32, (m2, 1), 0) + tpos0, 0, full_len - 1)
    for g, w in enumerate(POOL_WINDOWS):
        half = w // 2
        lanes = slice(g * pool_c, (g + 1) * pool_c)
        s = u_ref[hm:hm + m2, lanes]
        for d in range(-half, half):
            if d != 0:
                s = s + u_ref[hm + d:hm + d + m2, lanes]
        cnt = (jnp.minimum(tpos + half, full_len) - jnp.maximum(tpos - half, 0)).astype(_F32)
        p = s / cnt - u_ref[hm:hm + m2, lanes]
        q = _dot(p.astype(_BF16), pw_ref[g]) * ps_ref[:, lanes]
        cat_ref[:, d_a + g * pool_c:d_a + (g + 1) * pool_c] = q.astype(_BF16)
    return cat_ref


def _layer_kernel(prev_ref, x_ref, next_ref, meta_ref, *refs, mixer, n_mixer_params, hm,
                  seq_len, emit_meta):
    mixer_front, mixer_back = mixer
    mixer_params = refs[:n_mixer_params - 4]
    wout_ref, bout_ref, mlng_ref, mlnb_ref = refs[n_mixer_params - 4:n_mixer_params]
    ffn_params = refs[n_mixer_params:n_mixer_params + FFN_PARAMS]
    rest = refs[n_mixer_params + FFN_PARAMS:]
    o_ref = rest[0]
    meta_out_ref = rest[1] if emit_meta else None
    xe_ref, h1_ref, xf_ref, g_ref, af_ref, y0_ref = rest[1 + emit_meta:7 + emit_meta]
    mixer_scratch = rest[7 + emit_meta:]

    tl = x_ref.shape[1]
    hf = HALO_SMALL
    tiles_per_seq = seq_len // tl
    n_tiles = pl.num_programs(0) - 1
    g = pl.program_id(0)
    i_m = lax.rem(jnp.minimum(g, n_tiles - 1), jnp.int32(tiles_per_seq))
    i_f = lax.rem(jnp.maximum(g - 1, 0), jnp.int32(tiles_per_seq))
    m = tl + N_META
    m2 = m + 2 * hf
    ext = m2 + 2 * hm
    before = N_META + hf + hm

    @pl.when(g == 0)
    def _():
        h1_ref[...] = jnp.zeros(h1_ref.shape, _F32)

    xf_ref[...] = h1_ref[...]

    xe_ref[0:before, :] = prev_ref[0, PREV_ROWS - before:PREV_ROWS, :]
    xe_ref[hf + hm:before, :] = jnp.where(i_m == 0, meta_ref[0],
                                          prev_ref[0, PREV_ROWS - N_META:PREV_ROWS, :])
    xe_ref[before:before + tl, :] = x_ref[0]
    xe_ref[before + tl:ext, :] = next_ref[0, 0:hf + hm, :]
    first_pos = i_m * tl - before
    valid = _valid_rows(ext, first_pos, seq_len)
    mixer_front(xe_ref, valid, hm, m2, mixer_params, mixer_scratch)

    _ffn_body(xf_ref, i_f * tl - N_META - hf, seq_len, ffn_params, o_ref, y0_ref, g_ref, af_ref)

    lhs_ref = mixer_back(hm, m2, mixer_params, mixer_scratch,
                         tpos0=first_pos + hm + N_META, full_len=seq_len + N_META)

    def store_h1(s0, y):
        h1_ref[s0:s0 + LN_ROWS, :] = y

    _project_residual_ln(lhs_ref, wout_ref, bout_ref, mlng_ref, mlnb_ref, xe_ref, hm, store_h1)

    if emit_meta:
        @pl.when(i_f == 0)
        def _():
            meta_out_ref[0] = y0_ref[...]


def _resident(shape):
    zeros = (0,) * len(shape)
    return pl.BlockSpec(shape, lambda g: zeros, pipeline_mode=pl.Buffered(1))


def _row(v):
    return v.reshape(1, -1)


def _layer(x, meta, mixer, mixer_params, mixer_scratch, hm, ffn_params, emit_meta, name):
    bsz, seq_len, d_model = x.shape
    hf = HALO_SMALL
    assert seq_len % SEQ_TILE == 0 and SEQ_TILE % PREV_ROWS == 0 and SEQ_TILE % NEXT_ROWS == 0
    assert N_META + hf + hm <= PREV_ROWS and hf + hm <= NEXT_ROWS and N_META % LN_ROWS == 0
    assert meta.shape[1:] == (N_META, d_model) and len(ffn_params) == FFN_PARAMS
    tiles_per_seq = seq_len // SEQ_TILE
    n_tiles = bsz * tiles_per_seq
    prev_per_tile = SEQ_TILE // PREV_ROWS
    next_per_tile = SEQ_TILE // NEXT_ROWS
    last_next = seq_len // NEXT_ROWS - 1
    meta_batched = meta.shape[0] != 1

    def mixer_tile(g):
        return divmod(jnp.minimum(g, n_tiles - 1), tiles_per_seq)

    def ffn_tile(g):
        return divmod(jnp.maximum(g - 1, 0), tiles_per_seq)

    def prev_map(g):
        b, i = mixer_tile(g)
        return (b, jnp.maximum(i * prev_per_tile - 1, 0), 0)

    def tile_map(g):
        b, i = mixer_tile(g)
        return (b, i, 0)

    def next_map(g):
        b, i = mixer_tile(g)
        return (b, jnp.minimum((i + 1) * next_per_tile, last_next), 0)

    def out_map(g):
        b, i = ffn_tile(g)
        return (b, i, 0)

    in_specs = [
        pl.BlockSpec((1, PREV_ROWS, d_model), prev_map),
        pl.BlockSpec((1, SEQ_TILE, d_model), tile_map),
        pl.BlockSpec((1, NEXT_ROWS, d_model), next_map),
        (pl.BlockSpec((1, N_META, d_model), lambda g: (mixer_tile(g)[0], 0, 0))
         if meta_batched else _resident(meta.shape)),
    ] + [_resident(p.shape) for p in mixer_params + ffn_params]
    out_specs = [pl.BlockSpec((1, SEQ_TILE, d_model), out_map)]
    out_shape = [jax.ShapeDtypeStruct(x.shape, _F32)]
    if emit_meta:
        out_specs.append(pl.BlockSpec((1, N_META, d_model), lambda g: (ffn_tile(g)[0], 0, 0)))
        out_shape.append(jax.ShapeDtypeStruct((bsz, N_META, d_model), _F32))

    m = SEQ_TILE + N_META
    m2 = m + 2 * hf
    assert m % LN_ROWS == 0 and m2 % LN_ROWS == 0
    d_ff = ffn_params[4].shape[0]
    assert d_ff % CHUNK == 0
    scratch = [pltpu.VMEM((m2 + 2 * hm, d_model), _F32),
               pltpu.VMEM((m2, d_model), _F32),
               pltpu.VMEM((m2, d_model), _F32),
               pltpu.VMEM((2, m2, CHUNK), _F32),
               pltpu.VMEM((m, d_ff), _BF16),
               pltpu.VMEM((N_META, d_model), _F32)] + mixer_scratch
    outs = pl.pallas_call(
        functools.partial(_layer_kernel, mixer=mixer, n_mixer_params=len(mixer_params), hm=hm,
                          seq_len=seq_len, emit_meta=emit_meta),
        grid=(n_tiles + 1,),
        in_specs=in_specs,
        out_specs=out_specs,
        out_shape=out_shape,
        scratch_shapes=scratch,
        compiler_params=pltpu.CompilerParams(
            dimension_semantics=("arbitrary",),
            vmem_limit_bytes=VMEM_LIMIT_BYTES),
        name=name,
    )(x, x, x, meta, *mixer_params, *ffn_params)
    return (outs[0], outs[1]) if emit_meta else (outs[0], None)


def _ffn_param_list(w_up, b_up, conv_w, conv_b, w_down, b_down, ln_g, ln_b):
    return [w_up.astype(_BF16), _row(b_up), conv_w, _row(conv_b), w_down.astype(_BF16),
            _row(b_down), _row(ln_g), _row(ln_b)]


def _layer_ab(x, meta, w_in, b_in, conv_w, conv_b, n_g, n_b, pool_w, pool_scale, w_out, b_out,
              ln_g, ln_b, ffn_params, emit_meta):
    k_taps, d_a = conv_w.shape
    d_b = pool_scale.shape[0]
    hm = HALO_AB
    assert k_taps // 2 <= hm and max(POOL_WINDOWS) // 2 <= hm
    assert d_a % CHUNK == 0 and d_b % CHUNK == 0 and (d_a // LANES) % 2 == 0
    m2 = SEQ_TILE + N_META + 2 * HALO_SMALL
    ext = m2 + 2 * hm
    assert m2 % CONV_ROWS == 0 and CONV_ROWS % SUBLANES == 0
    conv_w_rows = jnp.broadcast_to(conv_w[:, None, :], (k_taps, SUBLANES, d_a))
    params = [w_in.astype(_BF16), _row(b_in), conv_w_rows, _row(conv_b), _row(n_g), _row(n_b),
              pool_w.astype(_BF16), _row(pool_scale), w_out.astype(_BF16), _row(b_out),
              _row(ln_g), _row(ln_b)]
    scratch = [pltpu.VMEM((ext, d_a), _F32),
               pltpu.VMEM((2, SUBLANES - 1, ext, LANES), _F32),
               pltpu.VMEM((ext, d_b), _F32),
               pltpu.VMEM((m2, d_a), _F32),
               pltpu.VMEM((m2, d_a + d_b), _BF16)]
    return _layer(x, meta, (_mixer_ab_front, _mixer_ab_back), params, scratch, hm, ffn_params,
                  emit_meta, "layer_ab")


def _layer_c(x, meta, w_in, b_in, conv_w, conv_b, w_out, b_out, ln_g, ln_b, ffn_params,
             emit_meta):
    d_c = w_out.shape[0]
    hm = HALO_SMALL
    assert d_c % CHUNK == 0
    m2 = SEQ_TILE + N_META + 2 * HALO_SMALL
    params = [w_in.astype(_BF16), _row(b_in), conv_w, _row(conv_b), w_out.astype(_BF16),
              _row(b_out), _row(ln_g), _row(ln_b)]
    scratch = [pltpu.VMEM((2, m2 + 2 * hm, CHUNK), _F32),
               pltpu.VMEM((m2, d_c), _BF16)]
    return _layer(x, meta, (_mixer_c_front, _mixer_c_back), params, scratch, hm, ffn_params,
                  emit_meta, "layer_c")


def kernel(x, meta_tokens, w_in_ab, b_in_ab, conv_a_w, conv_a_b, norm_a_g, norm_a_b, pool_w, pool_scale, w_out_ab, b_out_ab, w_in_c, b_in_c, conv_c_w, conv_c_b, w_out_c, b_out_c, mix_ln_g, mix_ln_b, ffn_w_up, ffn_b_up, ffn_conv_w, ffn_conv_b, ffn_w_down, ffn_b_down, ffn_ln_g, ffn_ln_b):
    assert meta_tokens.shape[0] == N_META and mix_ln_g.shape[0] == DEPTH
    h = x
    meta = meta_tokens[None].astype(x.dtype)
    for i in range(DEPTH):
        j = i // 2
        ffn_params = _ffn_param_list(ffn_w_up[i], ffn_b_up[i], ffn_conv_w[i], ffn_conv_b[i],
                                     ffn_w_down[i], ffn_b_down[i], ffn_ln_g[i], ffn_ln_b[i])
        emit_meta = i + 1 < DEPTH
        if i % 2 == 0:
            h, meta = _layer_ab(h, meta, w_in_ab[j], b_in_ab[j], conv_a_w[j], conv_a_b[j],
                                norm_a_g[j], norm_a_b[j], pool_w[j], pool_scale[j], w_out_ab[j],
                                b_out_ab[j], mix_ln_g[i], mix_ln_b[i], ffn_params, emit_meta)
        else:
            h, meta = _layer_c(h, meta, w_in_c[j], b_in_c[j], conv_c_w[j], conv_c_b[j],
                               w_out_c[j], b_out_c[j], mix_ln_g[i], mix_ln_b[i], ffn_params,
                               emit_meta)
    return h
```

```python
import functools
from typing import NamedTuple

import jax
import jax.numpy as jnp
from jax import lax
from jax.experimental import pallas as pl
from jax.experimental.pallas import tpu as pltpu

N_META = 16
POOL_WINDOWS = (2, 4, 8, 16)
LN_EPS = 1e-5
DEPTH = 2
DEEPNORM_ALPHA = (2.0 * DEPTH) ** 0.25

SUBLANES = 8
LANES = 128
SEQ_TILE = 512
PREV_ROWS = 64
NEXT_ROWS = 32
HALO_AB = 16
HALO_SMALL = 8
CHUNK = 256
LN_ROWS = 16
CONV_ROWS = 136
FFN_PARAMS = 8
WEIGHT_STAGE_BYTES = 3 * 512 * 1024
VMEM_LIMIT_BYTES = 60000 * 1024

_BF16 = jnp.bfloat16
_F32 = jnp.float32


def _dot(a, b):
    return jnp.dot(a, b, preferred_element_type=_F32)


def _layer_norm(x, g, b):
    mu = jnp.mean(x, axis=-1, keepdims=True)
    xc = x - mu
    var = jnp.mean(xc * xc, axis=-1, keepdims=True)
    return xc * lax.rsqrt(var + LN_EPS) * g + b


def _row_blocks(n_rows, n_blocks, multiple):
    units = n_rows // multiple
    bounds = [multiple * ((units * k) // n_blocks) for k in range(n_blocks + 1)]
    return list(zip(bounds[:-1], bounds[1:]))


class _HbmWeight(NamedTuple):
    stacked: jax.Array
    layer: int
    chunk_rows: int


def _load_weight_bf16(w_hbm, layer, chunk_rows, dst_ref):
    rows, cols = dst_ref.shape
    n_chunks = rows // chunk_rows

    def body(stage, sem):
        def copy(c):
            src = w_hbm.at[layer, pl.ds(c * chunk_rows, chunk_rows), :]
            return pltpu.make_async_copy(src, stage.at[c % 2], sem.at[c % 2])

        copy(0).start()
        for c in range(n_chunks):
            if c + 1 < n_chunks:
                copy(c + 1).start()
            copy(c).wait()
            dst_ref[c * chunk_rows:(c + 1) * chunk_rows, :] = stage[c % 2].astype(_BF16)

    pl.run_scoped(body, pltpu.VMEM((2, chunk_rows, cols), _F32),
                  pltpu.SemaphoreType.DMA((2,)))


def _valid_rows(n_rows, first_pos, seq_len):
    pos = lax.broadcasted_iota(jnp.int32, (n_rows, 1), 0) + first_pos
    return (pos >= -N_META) & (pos < seq_len)


def _project_residual_ln(lhs_ref, w_ref, bias_ref, g_ref, b_ref, res_ref, res_row0, store):
    bias = bias_ref[...]
    g = g_ref[...]
    b = b_ref[...]
    for r0, r1 in _row_blocks(lhs_ref.shape[0], 2, LN_ROWS):
        proj = _dot(lhs_ref[r0:r1, :], w_ref[...])
        for s0 in range(r0, r1, LN_ROWS):
            z = (DEEPNORM_ALPHA * res_ref[res_row0 + s0:res_row0 + s0 + LN_ROWS, :]
                 + (proj[s0 - r0:s0 - r0 + LN_ROWS, :] + bias))
            store(s0, _layer_norm(z, g, b))


def _ffn_body(xe_ref, first_pos, seq_len, params, o_ref, y0_ref, g_ref, a_ref):
    wup_ref, bup_ref, cw_ref, cb_ref, wdn_ref, bdn_ref, lng_ref, lnb_ref = params
    h = HALO_SMALL
    m = a_ref.shape[0]
    d_ff = wdn_ref.shape[0]

    xb = xe_ref[...].astype(_BF16)
    xbm = xe_ref[h:h + m, :].astype(_BF16)
    valid = _valid_rows(m + 2 * h, first_pos, seq_len)

    for j in range(d_ff // CHUNK):
        c0 = j * CHUNK
        gsl = slice(c0, c0 + CHUNK)
        vsl = slice(d_ff + c0, d_ff + c0 + CHUNK)
        g = _dot(xb, wup_ref[:, gsl]) + bup_ref[:, gsl]
        gbuf = g_ref.at[j % 2]
        gbuf[...] = jnp.where(valid, g, 0.0)
        v = _dot(xbm, wup_ref[:, vsl]) + bup_ref[:, vsl]
        gc = (cw_ref[1:2, gsl] * gbuf[h:h + m, :]
              + cw_ref[0:1, gsl] * gbuf[h - 1:h - 1 + m, :]
              + cw_ref[2:3, gsl] * gbuf[h + 1:h + 1 + m, :]
              + cb_ref[:, gsl])
        a_ref[:, gsl] = (jax.nn.silu(gc) * v).astype(_BF16)

    def store(s0, y):
        if s0 < N_META:
            y0_ref[s0:s0 + LN_ROWS, :] = y
        else:
            o_ref[0, s0 - N_META:s0 - N_META + LN_ROWS, :] = y

    _project_residual_ln(a_ref, wdn_ref, bdn_ref, lng_ref, lnb_ref, xe_ref, h, store)


def _mixer_c_front(xe_ref, valid, hm, m2, params, scratch):
    win_ref, bin_ref, cw_ref, cb_ref = params
    s_ref, y_ref = scratch
    d_c = y_ref.shape[1]
    xb = xe_ref[...].astype(_BF16)
    xbm = xe_ref[hm:hm + m2, :].astype(_BF16)

    for j in range(d_c // CHUNK):
        c0 = j * CHUNK
        bsl = slice(c0, c0 + CHUNK)
        csl = slice(d_c + c0, d_c + c0 + CHUNK)
        vsl = slice(2 * d_c + c0, 2 * d_c + c0 + CHUNK)
        cg = _dot(xb, win_ref[:, csl]) + bin_ref[:, csl]
        v = _dot(xb, win_ref[:, vsl]) + bin_ref[:, vsl]
        sbuf = s_ref.at[j % 2]
        sbuf[...] = jnp.where(valid, cg * v, 0.0)
        bg = _dot(xbm, win_ref[:, bsl]) + bin_ref[:, bsl]
        conv = (cw_ref[1:2, bsl] * sbuf[hm:hm + m2, :]
                + cw_ref[0:1, bsl] * sbuf[hm - 1:hm - 1 + m2, :]
                + cw_ref[2:3, bsl] * sbuf[hm + 1:hm + 1 + m2, :]
                + cb_ref[:, bsl])
        y_ref[:, bsl] = (bg * conv).astype(_BF16)


def _mixer_c_back(hm, m2, params, scratch, *, tpos0, full_len):
    del hm, m2, params, tpos0, full_len
    return scratch[1]


def _mixer_ab_front(xe_ref, valid, hm, m2, params, scratch):
    del hm, m2
    win_ref, bin_ref = params[:2]
    a_ref, _, u_ref = scratch[:3]
    d_a = a_ref.shape[1]
    d_b = u_ref.shape[1]
    xb = xe_ref[...].astype(_BF16)

    for c0 in range(0, d_a, CHUNK):
        vsl = slice(c0, c0 + CHUNK)
        gsl = slice(d_a + c0, d_a + c0 + CHUNK)
        a_val = _dot(xb, win_ref[:, vsl]) + bin_ref[:, vsl]
        a_gate = _dot(xb, win_ref[:, gsl]) + bin_ref[:, gsl]
        a_ref[:, vsl] = jnp.where(valid, a_val * jax.nn.sigmoid(a_gate), 0.0)
    for c0 in range(0, d_b, CHUNK):
        usl = slice(2 * d_a + c0, 2 * d_a + c0 + CHUNK)
        u = _dot(xb, win_ref[:, usl]) + bin_ref[:, usl]
        u_ref[:, c0:c0 + CHUNK] = jnp.where(valid, u, 0.0)


def _mixer_ab_back(hm, m2, params, scratch, *, tpos0, full_len):
    _, _, cwb_ref, cb_ref, ng_ref, nb_ref, pw_ref, ps_ref = params
    a_ref, ash_ref, u_ref, c_ref, cat_ref = scratch
    ext = m2 + 2 * hm
    d_a = a_ref.shape[1]
    d_b = u_ref.shape[1]
    pool_c = d_b // len(POOL_WINDOWS)
    k_taps = cwb_ref.shape[0]
    reach = k_taps // 2

    n_sh = ext - SUBLANES
    n_acc = CONV_ROWS // SUBLANES
    for lt in range(d_a // LANES):
        lanes = slice(lt * LANES, (lt + 1) * LANES)
        sh = ash_ref.at[lt % 2]
        for s in range(1, SUBLANES):
            sh[s - 1, 0:n_sh, :] = a_ref[s:s + n_sh, lanes]
        for r0 in range(0, m2, CONV_ROWS):
            accs = [None] * n_acc
            for k in range(k_taps):
                first_row = hm - reach + k
                s, q = first_row % SUBLANES, first_row // SUBLANES
                w = cwb_ref[k, :, lanes]
                for t in range(n_acc):
                    row = r0 + SUBLANES * (q + t)
                    blk = (a_ref[row:row + SUBLANES, lanes] if s == 0
                           else sh[s - 1, row:row + SUBLANES, :])
                    accs[t] = w * blk if accs[t] is None else accs[t] + w * blk
            for t, acc in enumerate(accs):
                c_ref[r0 + SUBLANES * t:r0 + SUBLANES * (t + 1), lanes] = acc + cb_ref[:, lanes]

    ng = ng_ref[...]
    nb = nb_ref[...]
    for r0 in range(0, m2, LN_ROWS):
        an = jax.nn.silu(_layer_norm(c_ref[r0:r0 + LN_ROWS, :], ng, nb))
        cat_ref[r0:r0 + LN_ROWS, 0:d_a] = an.astype(_BF16)

    tpos = jnp.clip(lax.broadcasted_iota(jnp.int32, (m2, 1), 0) + tpos0, 0, full_len - 1)
    for g, w in enumerate(POOL_WINDOWS):
        half = w // 2
        lanes = slice(g * pool_c, (g + 1) * pool_c)
        s = u_ref[hm:hm + m2, lanes]
        for d in range(-half, half):
            if d != 0:
                s = s + u_ref[hm + d:hm + d + m2, lanes]
        cnt = (jnp.minimum(tpos + half, full_len) - jnp.maximum(tpos - half, 0)).astype(_F32)
        p = s / cnt - u_ref[hm:hm + m2, lanes]
        q = _dot(p.astype(_BF16), pw_ref[g]) * ps_ref[:, lanes]
        cat_ref[:, d_a + g * pool_c:d_a + (g + 1) * pool_c] = q.astype(_BF16)
    return cat_ref


def _layer_kernel(prev_ref, x_ref, next_ref, meta_ref, *refs, mixer, n_mixer_params, hm,
                  seq_len, emit_meta, hbm_weights):
    mixer_front, mixer_back = mixer
    n_params = n_mixer_params + FFN_PARAMS
    n_hbm = len(hbm_weights)
    small = list(refs[:n_params - n_hbm])
    w_hbm = refs[n_params - n_hbm:n_params]
    rest = refs[n_params:]
    w_vmem = rest[len(rest) - n_hbm:]
    hbm_pos = [pos for pos, _, _ in hbm_weights]
    params = [w_vmem[hbm_pos.index(p)] if p in hbm_pos else small.pop(0)
              for p in range(n_params)]
    mixer_params = params[:n_mixer_params - 4]
    wout_ref, bout_ref, mlng_ref, mlnb_ref = params[n_mixer_params - 4:n_mixer_params]
    ffn_params = params[n_mixer_params:]
    o_ref = rest[0]
    meta_out_ref = rest[1] if emit_meta else None
    xe_ref, h1_ref, xf_ref, g_ref, af_ref, y0_ref = rest[1 + emit_meta:7 + emit_meta]
    mixer_scratch = rest[7 + emit_meta:len(rest) - n_hbm]

    tl = x_ref.shape[1]
    hf = HALO_SMALL
    tiles_per_seq = seq_len // tl
    n_tiles = pl.num_programs(0) - 1
    g = pl.program_id(0)
    i_m = lax.rem(jnp.minimum(g, n_tiles - 1), jnp.int32(tiles_per_seq))
    i_f = lax.rem(jnp.maximum(g - 1, 0), jnp.int32(tiles_per_seq))
    m = tl + N_META
    m2 = m + 2 * hf
    ext = m2 + 2 * hm
    before = N_META + hf + hm

    @pl.when(g == 0)
    def _():
        h1_ref[...] = jnp.zeros(h1_ref.shape, _F32)
        for k, (_, layer, chunk_rows) in enumerate(hbm_weights):
            _load_weight_bf16(w_hbm[k], layer, chunk_rows, w_vmem[k])

    xf_ref[...] = h1_ref[...]

    xe_ref[0:before, :] = prev_ref[0, PREV_ROWS - before:PREV_ROWS, :]
    xe_ref[hf + hm:before, :] = jnp.where(i_m == 0, meta_ref[0],
                                          prev_ref[0, PREV_ROWS - N_META:PREV_ROWS, :])
    xe_ref[before:before + tl, :] = x_ref[0]
    xe_ref[before + tl:ext, :] = next_ref[0, 0:hf + hm, :]
    first_pos = i_m * tl - before
    valid = _valid_rows(ext, first_pos, seq_len)
    mixer_front(xe_ref, valid, hm, m2, mixer_params, mixer_scratch)

    _ffn_body(xf_ref, i_f * tl - N_META - hf, seq_len, ffn_params, o_ref, y0_ref, g_ref, af_ref)

    lhs_ref = mixer_back(hm, m2, mixer_params, mixer_scratch,
                         tpos0=first_pos + hm + N_META, full_len=seq_len + N_META)

    def store_h1(s0, y):
        h1_ref[s0:s0 + LN_ROWS, :] = y

    _project_residual_ln(lhs_ref, wout_ref, bout_ref, mlng_ref, mlnb_ref, xe_ref, hm, store_h1)

    if emit_meta:
        @pl.when(i_f == 0)
        def _():
            meta_out_ref[0] = y0_ref[...]


def _resident(shape):
    zeros = (0,) * len(shape)
    return pl.BlockSpec(shape, lambda g: zeros, pipeline_mode=pl.Buffered(1))


def _row(v):
    return v.reshape(1, -1)


def _layer(x, meta, mixer, mixer_params, mixer_scratch, hm, ffn_params, emit_meta, name):
    bsz, seq_len, d_model = x.shape
    hf = HALO_SMALL
    assert seq_len % SEQ_TILE == 0 and SEQ_TILE % PREV_ROWS == 0 and SEQ_TILE % NEXT_ROWS == 0
    assert N_META + hf + hm <= PREV_ROWS and hf + hm <= NEXT_ROWS and N_META % LN_ROWS == 0
    assert meta.shape[1:] == (N_META, d_model) and len(ffn_params) == FFN_PARAMS
    tiles_per_seq = seq_len // SEQ_TILE
    n_tiles = bsz * tiles_per_seq
    prev_per_tile = SEQ_TILE // PREV_ROWS
    next_per_tile = SEQ_TILE // NEXT_ROWS
    last_next = seq_len // NEXT_ROWS - 1
    meta_batched = meta.shape[0] != 1

    def mixer_tile(g):
        return divmod(jnp.minimum(g, n_tiles - 1), tiles_per_seq)

    def ffn_tile(g):
        return divmod(jnp.maximum(g - 1, 0), tiles_per_seq)

    def prev_map(g):
        b, i = mixer_tile(g)
        return (b, jnp.maximum(i * prev_per_tile - 1, 0), 0)

    def tile_map(g):
        b, i = mixer_tile(g)
        return (b, i, 0)

    def next_map(g):
        b, i = mixer_tile(g)
        return (b, jnp.minimum((i + 1) * next_per_tile, last_next), 0)

    def out_map(g):
        b, i = ffn_tile(g)
        return (b, i, 0)

    params = mixer_params + ffn_params
    small = [p for p in params if not isinstance(p, _HbmWeight)]
    big = [p for p in params if isinstance(p, _HbmWeight)]
    hbm_weights = tuple((pos, p.layer, p.chunk_rows) for pos, p in enumerate(params)
                        if isinstance(p, _HbmWeight))
    for p in big:
        assert p.stacked.shape[1] % p.chunk_rows == 0 and p.chunk_rows % 16 == 0
    in_specs = [
        pl.BlockSpec((1, PREV_ROWS, d_model), prev_map),
        pl.BlockSpec((1, SEQ_TILE, d_model), tile_map),
        pl.BlockSpec((1, NEXT_ROWS, d_model), next_map),
        (pl.BlockSpec((1, N_META, d_model), lambda g: (mixer_tile(g)[0], 0, 0))
         if meta_batched else _resident(meta.shape)),
    ] + [_resident(p.shape) for p in small] + [pl.BlockSpec(memory_space=pl.ANY) for _ in big]
    out_specs = [pl.BlockSpec((1, SEQ_TILE, d_model), out_map)]
    out_shape = [jax.ShapeDtypeStruct(x.shape, _F32)]
    if emit_meta:
        out_specs.append(pl.BlockSpec((1, N_META, d_model), lambda g: (ffn_tile(g)[0], 0, 0)))
        out_shape.append(jax.ShapeDtypeStruct((bsz, N_META, d_model), _F32))

    m = SEQ_TILE + N_META
    m2 = m + 2 * hf
    assert m % LN_ROWS == 0 and m2 % LN_ROWS == 0
    d_ff = ffn_params[4].stacked.shape[1]
    assert d_ff % CHUNK == 0
    scratch = [pltpu.VMEM((m2 + 2 * hm, d_model), _F32),
               pltpu.VMEM((m2, d_model), _F32),
               pltpu.VMEM((m2, d_model), _F32),
               pltpu.VMEM((2, m2, CHUNK), _F32),
               pltpu.VMEM((m, d_ff), _BF16),
               pltpu.VMEM((N_META, d_model), _F32)] + mixer_scratch
    scratch += [pltpu.VMEM(p.stacked.shape[1:], _BF16) for p in big]
    outs = pl.pallas_call(
        functools.partial(_layer_kernel, mixer=mixer, n_mixer_params=len(mixer_params), hm=hm,
                          seq_len=seq_len, emit_meta=emit_meta, hbm_weights=hbm_weights),
        grid=(n_tiles + 1,),
        in_specs=in_specs,
        out_specs=out_specs,
        out_shape=out_shape,
        scratch_shapes=scratch,
        compiler_params=pltpu.CompilerParams(
            dimension_semantics=("arbitrary",),
            vmem_limit_bytes=VMEM_LIMIT_BYTES),
        name=name,
    )(x, x, x, meta, *small, *[p.stacked for p in big])
    return (outs[0], outs[1]) if emit_meta else (outs[0], None)


def _hbm_weight(stacked, layer):
    _, rows, cols = stacked.shape
    fits = [r for r in range(16, rows + 1, 16)
            if rows % r == 0 and r * cols * 4 <= WEIGHT_STAGE_BYTES]
    return _HbmWeight(stacked, layer, fits[-1])


def _ffn_param_list(w_up_all, layer, b_up, conv_w, conv_b, w_down_all, b_down, ln_g, ln_b):
    return [_hbm_weight(w_up_all, layer), _row(b_up), conv_w, _row(conv_b),
            _hbm_weight(w_down_all, layer), _row(b_down), _row(ln_g), _row(ln_b)]


def _layer_ab(x, meta, j, w_in_all, b_in, conv_w, conv_b, n_g, n_b, pool_w, pool_scale, w_out_all,
              b_out, ln_g, ln_b, ffn_params, emit_meta):
    k_taps, d_a = conv_w.shape
    d_b = pool_scale.shape[0]
    hm = HALO_AB
    assert k_taps // 2 <= hm and max(POOL_WINDOWS) // 2 <= hm
    assert d_a % CHUNK == 0 and d_b % CHUNK == 0 and (d_a // LANES) % 2 == 0
    m2 = SEQ_TILE + N_META + 2 * HALO_SMALL
    ext = m2 + 2 * hm
    assert m2 % CONV_ROWS == 0 and CONV_ROWS % SUBLANES == 0
    conv_w_rows = jnp.broadcast_to(conv_w[:, None, :], (k_taps, SUBLANES, d_a))
    params = [_hbm_weight(w_in_all, j), _row(b_in), conv_w_rows, _row(conv_b), _row(n_g),
              _row(n_b), pool_w.astype(_BF16), _row(pool_scale), _hbm_weight(w_out_all, j),
              _row(b_out), _row(ln_g), _row(ln_b)]
    scratch = [pltpu.VMEM((ext, d_a), _F32),
               pltpu.VMEM((2, SUBLANES - 1, ext, LANES), _F32),
               pltpu.VMEM((ext, d_b), _F32),
               pltpu.VMEM((m2, d_a), _F32),
               pltpu.VMEM((m2, d_a + d_b), _BF16)]
    return _layer(x, meta, (_mixer_ab_front, _mixer_ab_back), params, scratch, hm, ffn_params,
                  emit_meta, "layer_ab")


def _layer_c(x, meta, j, w_in_all, b_in, conv_w, conv_b, w_out_all, b_out, ln_g, ln_b,
             ffn_params, emit_meta):
    d_c = w_out_all.shape[1]
    hm = HALO_SMALL
    assert d_c % CHUNK == 0
    m2 = SEQ_TILE + N_META + 2 * HALO_SMALL
    params = [_hbm_weight(w_in_all, j), _row(b_in), conv_w, _row(conv_b),
              _hbm_weight(w_out_all, j), _row(b_out), _row(ln_g), _row(ln_b)]
    scratch = [pltpu.VMEM((2, m2 + 2 * hm, CHUNK), _F32),
               pltpu.VMEM((m2, d_c), _BF16)]
    return _layer(x, meta, (_mixer_c_front, _mixer_c_back), params, scratch, hm, ffn_params,
                  emit_meta, "layer_c")


def kernel(x, meta_tokens, w_in_ab, b_in_ab, conv_a_w, conv_a_b, norm_a_g, norm_a_b, pool_w, pool_scale, w_out_ab, b_out_ab, w_in_c, b_in_c, conv_c_w, conv_c_b, w_out_c, b_out_c, mix_ln_g, mix_ln_b, ffn_w_up, ffn_b_up, ffn_conv_w, ffn_conv_b, ffn_w_down, ffn_b_down, ffn_ln_g, ffn_ln_b):
    assert meta_tokens.shape[0] == N_META and mix_ln_g.shape[0] == DEPTH
    h = x
    meta = meta_tokens[None].astype(x.dtype)
    for i in range(DEPTH):
        j = i // 2
        ffn_params = _ffn_param_list(ffn_w_up, i, ffn_b_up[i], ffn_conv_w[i], ffn_conv_b[i],
                                     ffn_w_down, ffn_b_down[i], ffn_ln_g[i], ffn_ln_b[i])
        emit_meta = i + 1 < DEPTH
        if i % 2 == 0:
            h, meta = _layer_ab(h, meta, j, w_in_ab, b_in_ab[j], conv_a_w[j], conv_a_b[j],
                                norm_a_g[j], norm_a_b[j], pool_w[j], pool_scale[j], w_out_ab,
                                b_out_ab[j], mix_ln_g[i], mix_ln_b[i], ffn_params, emit_meta)
        else:
            h, meta = _layer_c(h, meta, j, w_in_c, b_in_c[j], conv_c_w[j], conv_c_b[j],
                               w_out_c, b_out_c[j], mix_ln_g[i], mix_ln_b[i], ffn_params,
                               emit_meta)
    return h
```

```python
import functools
from typing import NamedTuple

import jax
import jax.numpy as jnp
from jax import lax
from jax.experimental import pallas as pl
from jax.experimental.pallas import tpu as pltpu

N_META = 16
POOL_WINDOWS = (2, 4, 8, 16)
LN_EPS = 1e-5
DEPTH = 2
DEEPNORM_ALPHA = (2.0 * DEPTH) ** 0.25

SUBLANES = 8
LANES = 128
SEQ_TILE = 512
PREV_ROWS = 64
NEXT_ROWS = 32
HALO_AB = 16
HALO_SMALL = 8
CHUNK = 256
LN_ROWS = 16
CONV_ROWS = 136
FFN_PARAMS = 8
WEIGHT_STAGE_BYTES = 3 * 512 * 1024
WEIGHT_STAGE_SLOTS = 4
VMEM_LIMIT_BYTES = 60000 * 1024

_BF16 = jnp.bfloat16
_F32 = jnp.float32


def _dot(a, b):
    return jnp.dot(a, b, preferred_element_type=_F32)


def _layer_norm(x, g, b):
    mu = jnp.mean(x, axis=-1, keepdims=True)
    xc = x - mu
    var = jnp.mean(xc * xc, axis=-1, keepdims=True)
    return xc * lax.rsqrt(var + LN_EPS) * g + b


def _row_blocks(n_rows, n_blocks, multiple):
    units = n_rows // multiple
    bounds = [multiple * ((units * k) // n_blocks) for k in range(n_blocks + 1)]
    return list(zip(bounds[:-1], bounds[1:]))


class _HbmWeight(NamedTuple):
    stacked: jax.Array
    layer: int
    chunk_rows: int


def _load_weight_bf16(w_hbm, layer, chunk_rows, dst_ref):
    rows, cols = dst_ref.shape
    n_chunks = rows // chunk_rows
    n_slots = WEIGHT_STAGE_SLOTS

    def body(stage, sem):
        def copy(c):
            src = w_hbm.at[layer, pl.ds(c * chunk_rows, chunk_rows), :]
            return pltpu.make_async_copy(src, stage.at[c % n_slots], sem.at[c % n_slots])

        for c in range(min(n_slots - 1, n_chunks)):
            copy(c).start()
        for c in range(n_chunks):
            if c + n_slots - 1 < n_chunks:
                copy(c + n_slots - 1).start()
            copy(c).wait()
            dst_ref[c * chunk_rows:(c + 1) * chunk_rows, :] = stage[c % n_slots].astype(_BF16)

    pl.run_scoped(body, pltpu.VMEM((n_slots, chunk_rows, cols), _F32),
                  pltpu.SemaphoreType.DMA((n_slots,)))


def _valid_rows(n_rows, first_pos, seq_len):
    pos = lax.broadcasted_iota(jnp.int32, (n_rows, 1), 0) + first_pos
    return (pos >= -N_META) & (pos < seq_len)


def _project_residual_ln(lhs_ref, w_ref, bias_ref, g_ref, b_ref, res_ref, res_row0, store):
    bias = bias_ref[...]
    g = g_ref[...]
    b = b_ref[...]
    for r0, r1 in _row_blocks(lhs_ref.shape[0], 2, LN_ROWS):
        proj = _dot(lhs_ref[r0:r1, :], w_ref[...])
        for s0 in range(r0, r1, LN_ROWS):
            z = (DEEPNORM_ALPHA * res_ref[res_row0 + s0:res_row0 + s0 + LN_ROWS, :]
                 + (proj[s0 - r0:s0 - r0 + LN_ROWS, :] + bias))
            store(s0, _layer_norm(z, g, b))


def _ffn_body(xe_ref, first_pos, seq_len, params, o_ref, y0_ref, g_ref, a_ref):
    wup_ref, bup_ref, cw_ref, cb_ref, wdn_ref, bdn_ref, lng_ref, lnb_ref = params
    h = HALO_SMALL
    m = a_ref.shape[0]
    d_ff = wdn_ref.shape[0]

    xb = xe_ref[...].astype(_BF16)
    xbm = xe_ref[h:h + m, :].astype(_BF16)
    valid = _valid_rows(m + 2 * h, first_pos, seq_len)

    for j in range(d_ff // CHUNK):
        c0 = j * CHUNK
        gsl = slice(c0, c0 + CHUNK)
        vsl = slice(d_ff + c0, d_ff + c0 + CHUNK)
        g = _dot(xb, wup_ref[:, gsl]) + bup_ref[:, gsl]
        gbuf = g_ref.at[j % 2]
        gbuf[...] = jnp.where(valid, g, 0.0)
        v = _dot(xbm, wup_ref[:, vsl]) + bup_ref[:, vsl]
        gc = (cw_ref[1:2, gsl] * gbuf[h:h + m, :]
              + cw_ref[0:1, gsl] * gbuf[h - 1:h - 1 + m, :]
              + cw_ref[2:3, gsl] * gbuf[h + 1:h + 1 + m, :]
              + cb_ref[:, gsl])
        a_ref[:, gsl] = (jax.nn.silu(gc) * v).astype(_BF16)

    def store(s0, y):
        if s0 < N_META:
            y0_ref[s0:s0 + LN_ROWS, :] = y
        else:
            o_ref[0, s0 - N_META:s0 - N_META + LN_ROWS, :] = y

    _project_residual_ln(a_ref, wdn_ref, bdn_ref, lng_ref, lnb_ref, xe_ref, h, store)


def _mixer_c_front(xe_ref, valid, hm, m2, params, scratch):
    win_ref, bin_ref, cw_ref, cb_ref = params
    s_ref, y_ref = scratch
    d_c = y_ref.shape[1]
    xb = xe_ref[...].astype(_BF16)
    xbm = xe_ref[hm:hm + m2, :].astype(_BF16)

    for j in range(d_c // CHUNK):
        c0 = j * CHUNK
        bsl = slice(c0, c0 + CHUNK)
        csl = slice(d_c + c0, d_c + c0 + CHUNK)
        vsl = slice(2 * d_c + c0, 2 * d_c + c0 + CHUNK)
        cg = _dot(xb, win_ref[:, csl]) + bin_ref[:, csl]
        v = _dot(xb, win_ref[:, vsl]) + bin_ref[:, vsl]
        sbuf = s_ref.at[j % 2]
        sbuf[...] = jnp.where(valid, cg * v, 0.0)
        bg = _dot(xbm, win_ref[:, bsl]) + bin_ref[:, bsl]
        conv = (cw_ref[1:2, bsl] * sbuf[hm:hm + m2, :]
                + cw_ref[0:1, bsl] * sbuf[hm - 1:hm - 1 + m2, :]
                + cw_ref[2:3, bsl] * sbuf[hm + 1:hm + 1 + m2, :]
                + cb_ref[:, bsl])
        y_ref[:, bsl] = (bg * conv).astype(_BF16)


def _mixer_c_back(hm, m2, params, scratch, *, tpos0, full_len):
    del hm, m2, params, tpos0, full_len
    return scratch[1]


def _mixer_ab_front(xe_ref, valid, hm, m2, params, scratch):
    del hm, m2
    win_ref, bin_ref = params[:2]
    a_ref, _, u_ref = scratch[:3]
    d_a = a_ref.shape[1]
    d_b = u_ref.shape[1]
    xb = xe_ref[...].astype(_BF16)

    for c0 in range(0, d_a, CHUNK):
        vsl = slice(c0, c0 + CHUNK)
        gsl = slice(d_a + c0, d_a + c0 + CHUNK)
        a_val = _dot(xb, win_ref[:, vsl]) + bin_ref[:, vsl]
        a_gate = _dot(xb, win_ref[:, gsl]) + bin_ref[:, gsl]
        a_ref[:, vsl] = jnp.where(valid, a_val * jax.nn.sigmoid(a_gate), 0.0)
    for c0 in range(0, d_b, CHUNK):
        usl = slice(2 * d_a + c0, 2 * d_a + c0 + CHUNK)
        u = _dot(xb, win_ref[:, usl]) + bin_ref[:, usl]
        u_ref[:, c0:c0 + CHUNK] = jnp.where(valid, u, 0.0)


def _mixer_ab_back(hm, m2, params, scratch, *, tpos0, full_len):
    _, _, cwb_ref, cb_ref, ng_ref, nb_ref, pw_ref, ps_ref = params
    a_ref, ash_ref, u_ref, c_ref, cat_ref = scratch
    ext = m2 + 2 * hm
    d_a = a_ref.shape[1]
    d_b = u_ref.shape[1]
    pool_c = d_b // len(POOL_WINDOWS)
    k_taps = cwb_ref.shape[0]
    reach = k_taps // 2

    n_sh = ext - SUBLANES
    n_acc = CONV_ROWS // SUBLANES
    for lt in range(d_a // LANES):
        lanes = slice(lt * LANES, (lt + 1) * LANES)
        sh = ash_ref.at[lt % 2]
        for s in range(1, SUBLANES):
            sh[s - 1, 0:n_sh, :] = a_ref[s:s + n_sh, lanes]
        for r0 in range(0, m2, CONV_ROWS):
            accs = [None] * n_acc
            for k in range(k_taps):
                first_row = hm - reach + k
                s, q = first_row % SUBLANES, first_row // SUBLANES
                w = cwb_ref[k, :, lanes]
                for t in range(n_acc):
                    row = r0 + SUBLANES * (q + t)
                    blk = (a_ref[row:row + SUBLANES, lanes] if s == 0
                           else sh[s - 1, row:row + SUBLANES, :])
                    accs[t] = w * blk if accs[t] is None else accs[t] + w * blk
            for t, acc in enumerate(accs):
                c_ref[r0 + SUBLANES * t:r0 + SUBLANES * (t + 1), lanes] = acc + cb_ref[:, lanes]

    ng = ng_ref[...]
    nb = nb_ref[...]
    for r0 in range(0, m2, LN_ROWS):
        an = jax.nn.silu(_layer_norm(c_ref[r0:r0 + LN_ROWS, :], ng, nb))
        cat_ref[r0:r0 + LN_ROWS, 0:d_a] = an.astype(_BF16)

    tpos = jnp.clip(lax.broadcasted_iota(jnp.int32, (m2, 1), 0) + tpos0, 0, full_len - 1)
    for g, w in enumerate(POOL_WINDOWS):
        half = w // 2
        lanes = slice(g * pool_c, (g + 1) * pool_c)
        s = u_ref[hm:hm + m2, lanes]
        for d in range(-half, half):
            if d != 0:
                s = s + u_ref[hm + d:hm + d + m2, lanes]
        cnt = (jnp.minimum(tpos + half, full_len) - jnp.maximum(tpos - half, 0)).astype(_F32)
        p = s / cnt - u_ref[hm:hm + m2, lanes]
        q = _dot(p.astype(_BF16), pw_ref[g]) * ps_ref[:, lanes]
        cat_ref[:, d_a + g * pool_c:d_a + (g + 1) * pool_c] = q.astype(_BF16)
    return cat_ref


def _layer_kernel(prev_ref, x_ref, next_ref, meta_ref, *refs, mixer, n_mixer_params, hm,
                  seq_len, emit_meta, hbm_weights):
    mixer_front, mixer_back = mixer
    n_params = n_mixer_params + FFN_PARAMS
    n_hbm = len(hbm_weights)
    small = list(refs[:n_params - n_hbm])
    w_hbm = refs[n_params - n_hbm:n_params]
    rest = refs[n_params:]
    w_vmem = rest[len(rest) - n_hbm:]
    hbm_pos = [pos for pos, _, _ in hbm_weights]
    params = [w_vmem[hbm_pos.index(p)] if p in hbm_pos else small.pop(0)
              for p in range(n_params)]
    mixer_params = params[:n_mixer_params - 4]
    wout_ref, bout_ref, mlng_ref, mlnb_ref = params[n_mixer_params - 4:n_mixer_params]
    ffn_params = params[n_mixer_params:]
    o_ref = rest[0]
    meta_out_ref = rest[1] if emit_meta else None
    xe_ref, h1_ref, xf_ref, g_ref, af_ref, y0_ref = rest[1 + emit_meta:7 + emit_meta]
    mixer_scratch = rest[7 + emit_meta:len(rest) - n_hbm]

    tl = x_ref.shape[1]
    hf = HALO_SMALL
    tiles_per_seq = seq_len // tl
    n_tiles = pl.num_programs(0) - 1
    g = pl.program_id(0)
    i_m = lax.rem(jnp.minimum(g, n_tiles - 1), jnp.int32(tiles_per_seq))
    i_f = lax.rem(jnp.maximum(g - 1, 0), jnp.int32(tiles_per_seq))
    m = tl + N_META
    m2 = m + 2 * hf
    ext = m2 + 2 * hm
    before = N_META + hf + hm

    @pl.when(g == 0)
    def _():
        h1_ref[...] = jnp.zeros(h1_ref.shape, _F32)
        for k, (_, layer, chunk_rows) in enumerate(hbm_weights):
            _load_weight_bf16(w_hbm[k], layer, chunk_rows, w_vmem[k])

    xf_ref[...] = h1_ref[...]

    xe_ref[0:before, :] = prev_ref[0, PREV_ROWS - before:PREV_ROWS, :]
    xe_ref[hf + hm:before, :] = jnp.where(i_m == 0, meta_ref[0],
                                          prev_ref[0, PREV_ROWS - N_META:PREV_ROWS, :])
    xe_ref[before:before + tl, :] = x_ref[0]
    xe_ref[before + tl:ext, :] = next_ref[0, 0:hf + hm, :]
    first_pos = i_m * tl - before
    valid = _valid_rows(ext, first_pos, seq_len)
    mixer_front(xe_ref, valid, hm, m2, mixer_params, mixer_scratch)

    _ffn_body(xf_ref, i_f * tl - N_META - hf, seq_len, ffn_params, o_ref, y0_ref, g_ref, af_ref)

    lhs_ref = mixer_back(hm, m2, mixer_params, mixer_scratch,
                         tpos0=first_pos + hm + N_META, full_len=seq_len + N_META)

    def store_h1(s0, y):
        h1_ref[s0:s0 + LN_ROWS, :] = y

    _project_residual_ln(lhs_ref, wout_ref, bout_ref, mlng_ref, mlnb_ref, xe_ref, hm, store_h1)

    if emit_meta:
        @pl.when(i_f == 0)
        def _():
            meta_out_ref[0] = y0_ref[...]


def _resident(shape):
    zeros = (0,) * len(shape)
    return pl.BlockSpec(shape, lambda g: zeros, pipeline_mode=pl.Buffered(1))


def _row(v):
    return v.reshape(1, -1)


def _layer(x, meta, mixer, mixer_params, mixer_scratch, hm, ffn_params, emit_meta, name):
    bsz, seq_len, d_model = x.shape
    hf = HALO_SMALL
    assert seq_len % SEQ_TILE == 0 and SEQ_TILE % PREV_ROWS == 0 and SEQ_TILE % NEXT_ROWS == 0
    assert N_META + hf + hm <= PREV_ROWS and hf + hm <= NEXT_ROWS and N_META % LN_ROWS == 0
    assert meta.shape[1:] == (N_META, d_model) and len(ffn_params) == FFN_PARAMS
    tiles_per_seq = seq_len // SEQ_TILE
    n_tiles = bsz * tiles_per_seq
    prev_per_tile = SEQ_TILE // PREV_ROWS
    next_per_tile = SEQ_TILE // NEXT_ROWS
    last_next = seq_len // NEXT_ROWS - 1
    meta_batched = meta.shape[0] != 1

    def mixer_tile(g):
        return divmod(jnp.minimum(g, n_tiles - 1), tiles_per_seq)

    def ffn_tile(g):
        return divmod(jnp.maximum(g - 1, 0), tiles_per_seq)

    def prev_map(g):
        b, i = mixer_tile(g)
        return (b, jnp.maximum(i * prev_per_tile - 1, 0), 0)

    def tile_map(g):
        b, i = mixer_tile(g)
        return (b, i, 0)

    def next_map(g):
        b, i = mixer_tile(g)
        return (b, jnp.minimum((i + 1) * next_per_tile, last_next), 0)

    def out_map(g):
        b, i = ffn_tile(g)
        return (b, i, 0)

    params = mixer_params + ffn_params
    small = [p for p in params if not isinstance(p, _HbmWeight)]
    big = [p for p in params if isinstance(p, _HbmWeight)]
    hbm_weights = tuple((pos, p.layer, p.chunk_rows) for pos, p in enumerate(params)
                        if isinstance(p, _HbmWeight))
    for p in big:
        assert p.stacked.shape[1] % p.chunk_rows == 0 and p.chunk_rows % 16 == 0
    in_specs = [
        pl.BlockSpec((1, PREV_ROWS, d_model), prev_map),
        pl.BlockSpec((1, SEQ_TILE, d_model), tile_map),
        pl.BlockSpec((1, NEXT_ROWS, d_model), next_map),
        (pl.BlockSpec((1, N_META, d_model), lambda g: (mixer_tile(g)[0], 0, 0))
         if meta_batched else _resident(meta.shape)),
    ] + [_resident(p.shape) for p in small] + [pl.BlockSpec(memory_space=pl.ANY) for _ in big]
    out_specs = [pl.BlockSpec((1, SEQ_TILE, d_model), out_map)]
    out_shape = [jax.ShapeDtypeStruct(x.shape, _F32)]
    if emit_meta:
        out_specs.append(pl.BlockSpec((1, N_META, d_model), lambda g: (ffn_tile(g)[0], 0, 0)))
        out_shape.append(jax.ShapeDtypeStruct((bsz, N_META, d_model), _F32))

    m = SEQ_TILE + N_META
    m2 = m + 2 * hf
    assert m % LN_ROWS == 0 and m2 % LN_ROWS == 0
    d_ff = ffn_params[4].stacked.shape[1]
    assert d_ff % CHUNK == 0
    scratch = [pltpu.VMEM((m2 + 2 * hm, d_model), _F32),
               pltpu.VMEM((m2, d_model), _F32),
               pltpu.VMEM((m2, d_model), _F32),
               pltpu.VMEM((2, m2, CHUNK), _F32),
               pltpu.VMEM((m, d_ff), _BF16),
               pltpu.VMEM((N_META, d_model), _F32)] + mixer_scratch
    scratch += [pltpu.VMEM(p.stacked.shape[1:], _BF16) for p in big]
    outs = pl.pallas_call(
        functools.partial(_layer_kernel, mixer=mixer, n_mixer_params=len(mixer_params), hm=hm,
                          seq_len=seq_len, emit_meta=emit_meta, hbm_weights=hbm_weights),
        grid=(n_tiles + 1,),
        in_specs=in_specs,
        out_specs=out_specs,
        out_shape=out_shape,
        scratch_shapes=scratch,
        compiler_params=pltpu.CompilerParams(
            dimension_semantics=("arbitrary",),
            vmem_limit_bytes=VMEM_LIMIT_BYTES),
        name=name,
    )(x, x, x, meta, *small, *[p.stacked for p in big])
    return (outs[0], outs[1]) if emit_meta else (outs[0], None)


def _hbm_weight(stacked, layer):
    _, rows, cols = stacked.shape
    fits = [r for r in range(16, rows + 1, 16)
            if rows % r == 0 and r * cols * 4 <= WEIGHT_STAGE_BYTES]
    return _HbmWeight(stacked, layer, fits[-1])


def _ffn_param_list(w_up_all, layer, b_up, conv_w, conv_b, w_down_all, b_down, ln_g, ln_b):
    return [_hbm_weight(w_up_all, layer), _row(b_up), conv_w, _row(conv_b),
            _hbm_weight(w_down_all, layer), _row(b_down), _row(ln_g), _row(ln_b)]


def _layer_ab(x, meta, j, w_in_all, b_in, conv_w, conv_b, n_g, n_b, pool_w, pool_scale, w_out_all,
              b_out, ln_g, ln_b, ffn_params, emit_meta):
    k_taps, d_a = conv_w.shape
    d_b = pool_scale.shape[0]
    hm = HALO_AB
    assert k_taps // 2 <= hm and max(POOL_WINDOWS) // 2 <= hm
    assert d_a % CHUNK == 0 and d_b % CHUNK == 0 and (d_a // LANES) % 2 == 0
    m2 = SEQ_TILE + N_META + 2 * HALO_SMALL
    ext = m2 + 2 * hm
    assert m2 % CONV_ROWS == 0 and CONV_ROWS % SUBLANES == 0
    conv_w_rows = jnp.broadcast_to(conv_w[:, None, :], (k_taps, SUBLANES, d_a))
    params = [_hbm_weight(w_in_all, j), _row(b_in), conv_w_rows, _row(conv_b), _row(n_g),
              _row(n_b), pool_w.astype(_BF16), _row(pool_scale), _hbm_weight(w_out_all, j),
              _row(b_out), _row(ln_g), _row(ln_b)]
    scratch = [pltpu.VMEM((ext, d_a), _F32),
               pltpu.VMEM((2, SUBLANES - 1, ext, LANES), _F32),
               pltpu.VMEM((ext, d_b), _F32),
               pltpu.VMEM((m2, d_a), _F32),
               pltpu.VMEM((m2, d_a + d_b), _BF16)]
    return _layer(x, meta, (_mixer_ab_front, _mixer_ab_back), params, scratch, hm, ffn_params,
                  emit_meta, "layer_ab")


def _layer_c(x, meta, j, w_in_all, b_in, conv_w, conv_b, w_out_all, b_out, ln_g, ln_b,
             ffn_params, emit_meta):
    d_c = w_out_all.shape[1]
    hm = HALO_SMALL
    assert d_c % CHUNK == 0
    m2 = SEQ_TILE + N_META + 2 * HALO_SMALL
    params = [_hbm_weight(w_in_all, j), _row(b_in), conv_w, _row(conv_b),
              _hbm_weight(w_out_all, j), _row(b_out), _row(ln_g), _row(ln_b)]
    scratch = [pltpu.VMEM((2, m2 + 2 * hm, CHUNK), _F32),
               pltpu.VMEM((m2, d_c), _BF16)]
    return _layer(x, meta, (_mixer_c_front, _mixer_c_back), params, scratch, hm, ffn_params,
                  emit_meta, "layer_c")


def kernel(x, meta_tokens, w_in_ab, b_in_ab, conv_a_w, conv_a_b, norm_a_g, norm_a_b, pool_w, pool_scale, w_out_ab, b_out_ab, w_in_c, b_in_c, conv_c_w, conv_c_b, w_out_c, b_out_c, mix_ln_g, mix_ln_b, ffn_w_up, ffn_b_up, ffn_conv_w, ffn_conv_b, ffn_w_down, ffn_b_down, ffn_ln_g, ffn_ln_b):
    assert meta_tokens.shape[0] == N_META and mix_ln_g.shape[0] == DEPTH
    h = x
    meta = meta_tokens[None].astype(x.dtype)
    for i in range(DEPTH):
        j = i // 2
        ffn_params = _ffn_param_list(ffn_w_up, i, ffn_b_up[i], ffn_conv_w[i], ffn_conv_b[i],
                                     ffn_w_down, ffn_b_down[i], ffn_ln_g[i], ffn_ln_b[i])
        emit_meta = i + 1 < DEPTH
        if i % 2 == 0:
            h, meta = _layer_ab(h, meta, j, w_in_ab, b_in_ab[j], conv_a_w[j], conv_a_b[j],
                                norm_a_g[j], norm_a_b[j], pool_w[j], pool_scale[j], w_out_ab,
                                b_out_ab[j], mix_ln_g[i], mix_ln_b[i], ffn_params, emit_meta)
        else:
            h, meta = _layer_c(h, meta, j, w_in_c, b_in_c[j], conv_c_w[j], conv_c_b[j],
                               w_out_c, b_out_c[j], mix_ln_g[i], mix_ln_b[i], ffn_params,
                               emit_meta)
    return h
```

```python
import functools
from typing import NamedTuple

import jax
import jax.numpy as jnp
from jax import lax
from jax.experimental import pallas as pl
from jax.experimental.pallas import tpu as pltpu

N_META = 16
POOL_WINDOWS = (2, 4, 8, 16)
LN_EPS = 1e-5
DEPTH = 2
DEEPNORM_ALPHA = (2.0 * DEPTH) ** 0.25

SUBLANES = 8
LANES = 128
SEQ_TILE = 512
PREV_ROWS = 64
NEXT_ROWS = 32
HALO_AB = 16
HALO_SMALL = 8
CHUNK = 256
LN_ROWS = 16
CONV_ROWS = 32
FFN_PARAMS = 8
WEIGHT_STAGE_BYTES = 3 * 512 * 1024
WEIGHT_STAGE_SLOTS = 4
VMEM_LIMIT_BYTES = 60000 * 1024

_BF16 = jnp.bfloat16
_F32 = jnp.float32


def _dot(a, b):
    return jnp.dot(a, b, preferred_element_type=_F32)


def _layer_norm(x, g, b):
    mu = jnp.mean(x, axis=-1, keepdims=True)
    xc = x - mu
    var = jnp.mean(xc * xc, axis=-1, keepdims=True)
    return xc * lax.rsqrt(var + LN_EPS) * g + b


def _row_blocks(n_rows, n_blocks, multiple):
    units = n_rows // multiple
    bounds = [multiple * ((units * k) // n_blocks) for k in range(n_blocks + 1)]
    return list(zip(bounds[:-1], bounds[1:]))


class _HbmWeight(NamedTuple):
    stacked: jax.Array
    layer: int
    chunk_rows: int


def _load_weight_bf16(w_hbm, layer, chunk_rows, dst_ref):
    rows, cols = dst_ref.shape
    n_chunks = rows // chunk_rows
    n_slots = WEIGHT_STAGE_SLOTS

    def body(stage, sem):
        def copy(c):
            src = w_hbm.at[layer, pl.ds(c * chunk_rows, chunk_rows), :]
            return pltpu.make_async_copy(src, stage.at[c % n_slots], sem.at[c % n_slots])

        for c in range(min(n_slots - 1, n_chunks)):
            copy(c).start()
        for c in range(n_chunks):
            if c + n_slots - 1 < n_chunks:
                copy(c + n_slots - 1).start()
            copy(c).wait()
            dst_ref[c * chunk_rows:(c + 1) * chunk_rows, :] = stage[c % n_slots].astype(_BF16)

    pl.run_scoped(body, pltpu.VMEM((n_slots, chunk_rows, cols), _F32),
                  pltpu.SemaphoreType.DMA((n_slots,)))


def _valid_rows(n_rows, first_pos, seq_len):
    pos = lax.broadcasted_iota(jnp.int32, (n_rows, 1), 0) + first_pos
    return (pos >= -N_META) & (pos < seq_len)


def _project_residual_ln(lhs_ref, w_ref, bias_ref, g_ref, b_ref, res_ref, res_row0, store):
    bias = bias_ref[...]
    g = g_ref[...]
    b = b_ref[...]
    for r0, r1 in _row_blocks(lhs_ref.shape[0], 2, LN_ROWS):
        proj = _dot(lhs_ref[r0:r1, :], w_ref[...])
        for s0 in range(r0, r1, LN_ROWS):
            z = (DEEPNORM_ALPHA * res_ref[res_row0 + s0:res_row0 + s0 + LN_ROWS, :]
                 + (proj[s0 - r0:s0 - r0 + LN_ROWS, :] + bias))
            store(s0, _layer_norm(z, g, b))


def _ffn_body(xe_ref, first_pos, seq_len, params, o_ref, y0_ref, g_ref, a_ref):
    wup_ref, bup_ref, cw_ref, cb_ref, wdn_ref, bdn_ref, lng_ref, lnb_ref = params
    h = HALO_SMALL
    m = a_ref.shape[0]
    d_ff = wdn_ref.shape[0]

    xb = xe_ref[...].astype(_BF16)
    xbm = xe_ref[h:h + m, :].astype(_BF16)
    valid = _valid_rows(m + 2 * h, first_pos, seq_len)

    for j in range(d_ff // CHUNK):
        c0 = j * CHUNK
        gsl = slice(c0, c0 + CHUNK)
        vsl = slice(d_ff + c0, d_ff + c0 + CHUNK)
        g = _dot(xb, wup_ref[:, gsl]) + bup_ref[:, gsl]
        gbuf = g_ref.at[j % 2]
        gbuf[...] = jnp.where(valid, g, 0.0)
        v = _dot(xbm, wup_ref[:, vsl]) + bup_ref[:, vsl]
        gc = (cw_ref[1:2, gsl] * gbuf[h:h + m, :]
              + cw_ref[0:1, gsl] * gbuf[h - 1:h - 1 + m, :]
              + cw_ref[2:3, gsl] * gbuf[h + 1:h + 1 + m, :]
              + cb_ref[:, gsl])
        a_ref[:, gsl] = (jax.nn.silu(gc) * v).astype(_BF16)

    def store(s0, y):
        if s0 < N_META:
            y0_ref[s0:s0 + LN_ROWS, :] = y
        else:
            o_ref[0, s0 - N_META:s0 - N_META + LN_ROWS, :] = y

    _project_residual_ln(a_ref, wdn_ref, bdn_ref, lng_ref, lnb_ref, xe_ref, h, store)


def _mixer_c_front(xe_ref, valid, hm, m2, params, scratch):
    win_ref, bin_ref, cw_ref, cb_ref = params
    s_ref, y_ref = scratch
    d_c = y_ref.shape[1]
    xb = xe_ref[...].astype(_BF16)
    xbm = xe_ref[hm:hm + m2, :].astype(_BF16)

    for j in range(d_c // CHUNK):
        c0 = j * CHUNK
        bsl = slice(c0, c0 + CHUNK)
        csl = slice(d_c + c0, d_c + c0 + CHUNK)
        vsl = slice(2 * d_c + c0, 2 * d_c + c0 + CHUNK)
        cg = _dot(xb, win_ref[:, csl]) + bin_ref[:, csl]
        v = _dot(xb, win_ref[:, vsl]) + bin_ref[:, vsl]
        sbuf = s_ref.at[j % 2]
        sbuf[...] = jnp.where(valid, cg * v, 0.0)
        bg = _dot(xbm, win_ref[:, bsl]) + bin_ref[:, bsl]
        conv = (cw_ref[1:2, bsl] * sbuf[hm:hm + m2, :]
                + cw_ref[0:1, bsl] * sbuf[hm - 1:hm - 1 + m2, :]
                + cw_ref[2:3, bsl] * sbuf[hm + 1:hm + 1 + m2, :]
                + cb_ref[:, bsl])
        y_ref[:, bsl] = (bg * conv).astype(_BF16)


def _mixer_c_back(hm, m2, params, scratch, *, tpos0, full_len):
    del hm, m2, params, tpos0, full_len
    return scratch[1]


def _mixer_ab_front(xe_ref, valid, hm, m2, params, scratch):
    del hm, m2
    win_ref, bin_ref = params[:2]
    a_ref, _, u_ref = scratch[:3]
    d_a = a_ref.shape[1]
    d_b = u_ref.shape[1]
    xb = xe_ref[...].astype(_BF16)

    for c0 in range(0, d_a, CHUNK):
        vsl = slice(c0, c0 + CHUNK)
        gsl = slice(d_a + c0, d_a + c0 + CHUNK)
        a_val = _dot(xb, win_ref[:, vsl]) + bin_ref[:, vsl]
        a_gate = _dot(xb, win_ref[:, gsl]) + bin_ref[:, gsl]
        a_ref[:, vsl] = jnp.where(valid, a_val * jax.nn.sigmoid(a_gate), 0.0)
    for c0 in range(0, d_b, CHUNK):
        usl = slice(2 * d_a + c0, 2 * d_a + c0 + CHUNK)
        u = _dot(xb, win_ref[:, usl]) + bin_ref[:, usl]
        u_ref[:, c0:c0 + CHUNK] = jnp.where(valid, u, 0.0)


def _mixer_ab_back(hm, m2, params, scratch, *, tpos0, full_len):
    _, _, cwb_ref, cb_ref, ng_ref, nb_ref, pw_ref, ps_ref = params
    a_ref, ash_ref, u_ref, c_ref, cat_ref = scratch
    ext = m2 + 2 * hm
    d_a = a_ref.shape[1]
    d_b = u_ref.shape[1]
    pool_c = d_b // len(POOL_WINDOWS)
    k_taps = cwb_ref.shape[0]
    reach = k_taps // 2

    n_sh = ext - SUBLANES
    n_acc = CONV_ROWS // SUBLANES
    for lt in range(d_a // LANES):
        lanes = slice(lt * LANES, (lt + 1) * LANES)
        sh = ash_ref.at[lt % 2]
        for s in range(1, SUBLANES):
            sh[s - 1, 0:n_sh, :] = a_ref[s:s + n_sh, lanes]
        for r0 in range(0, m2, CONV_ROWS):
            accs = [None] * n_acc
            for k in range(k_taps):
                first_row = hm - reach + k
                s, q = first_row % SUBLANES, first_row // SUBLANES
                w = cwb_ref[k, :, lanes]
                for t in range(n_acc):
                    row = r0 + SUBLANES * (q + t)
                    blk = (a_ref[row:row + SUBLANES, lanes] if s == 0
                           else sh[s - 1, row:row + SUBLANES, :])
                    accs[t] = w * blk if accs[t] is None else accs[t] + w * blk
            for t, acc in enumerate(accs):
                c_ref[r0 + SUBLANES * t:r0 + SUBLANES * (t + 1), lanes] = acc + cb_ref[:, lanes]

    ng = ng_ref[...]
    nb = nb_ref[...]
    for r0 in range(0, m2, LN_ROWS):
        an = jax.nn.silu(_layer_norm(c_ref[r0:r0 + LN_ROWS, :], ng, nb))
        cat_ref[r0:r0 + LN_ROWS, 0:d_a] = an.astype(_BF16)

    tpos = jnp.clip(lax.broadcasted_iota(jnp.int32, (m2, 1), 0) + tpos0, 0, full_len - 1)
    for g, w in enumerate(POOL_WINDOWS):
        half = w // 2
        lanes = slice(g * pool_c, (g + 1) * pool_c)
        s = u_ref[hm:hm + m2, lanes]
        for d in range(-half, half):
            if d != 0:
                s = s + u_ref[hm + d:hm + d + m2, lanes]
        cnt = (jnp.minimum(tpos + half, full_len) - jnp.maximum(tpos - half, 0)).astype(_F32)
        p = s / cnt - u_ref[hm:hm + m2, lanes]
        q = _dot(p.astype(_BF16), pw_ref[g]) * ps_ref[:, lanes]
        cat_ref[:, d_a + g * pool_c:d_a + (g + 1) * pool_c] = q.astype(_BF16)
    return cat_ref


def _layer_kernel(prev_ref, x_ref, next_ref, meta_ref, *refs, mixer, n_mixer_params, hm,
                  seq_len, emit_meta, hbm_weights):
    mixer_front, mixer_back = mixer
    n_params = n_mixer_params + FFN_PARAMS
    n_hbm = len(hbm_weights)
    small = list(refs[:n_params - n_hbm])
    w_hbm = refs[n_params - n_hbm:n_params]
    rest = refs[n_params:]
    w_vmem = rest[len(rest) - n_hbm:]
    hbm_pos = [pos for pos, _, _ in hbm_weights]
    params = [w_vmem[hbm_pos.index(p)] if p in hbm_pos else small.pop(0)
              for p in range(n_params)]
    mixer_params = params[:n_mixer_params - 4]
    wout_ref, bout_ref, mlng_ref, mlnb_ref = params[n_mixer_params - 4:n_mixer_params]
    ffn_params = params[n_mixer_params:]
    o_ref = rest[0]
    meta_out_ref = rest[1] if emit_meta else None
    xe_ref, h1_ref, xf_ref, g_ref, af_ref, y0_ref = rest[1 + emit_meta:7 + emit_meta]
    mixer_scratch = rest[7 + emit_meta:len(rest) - n_hbm]

    tl = x_ref.shape[1]
    hf = HALO_SMALL
    tiles_per_seq = seq_len // tl
    n_tiles = pl.num_programs(0) - 1
    g = pl.program_id(0)
    i_m = lax.rem(jnp.minimum(g, n_tiles - 1), jnp.int32(tiles_per_seq))
    i_f = lax.rem(jnp.maximum(g - 1, 0), jnp.int32(tiles_per_seq))
    m = tl + N_META
    m2 = m + 2 * hf
    ext = m2 + 2 * hm
    before = N_META + hf + hm

    @pl.when(g == 0)
    def _():
        h1_ref[...] = jnp.zeros(h1_ref.shape, _F32)
        for k, (_, layer, chunk_rows) in enumerate(hbm_weights):
            _load_weight_bf16(w_hbm[k], layer, chunk_rows, w_vmem[k])

    xf_ref[...] = h1_ref[...]

    xe_ref[0:before, :] = prev_ref[0, PREV_ROWS - before:PREV_ROWS, :]
    xe_ref[hf + hm:before, :] = jnp.where(i_m == 0, meta_ref[0],
                                          prev_ref[0, PREV_ROWS - N_META:PREV_ROWS, :])
    xe_ref[before:before + tl, :] = x_ref[0]
    xe_ref[before + tl:ext, :] = next_ref[0, 0:hf + hm, :]
    first_pos = i_m * tl - before
    valid = _valid_rows(ext, first_pos, seq_len)
    mixer_front(xe_ref, valid, hm, m2, mixer_params, mixer_scratch)

    _ffn_body(xf_ref, i_f * tl - N_META - hf, seq_len, ffn_params, o_ref, y0_ref, g_ref, af_ref)

    lhs_ref = mixer_back(hm, m2, mixer_params, mixer_scratch,
                         tpos0=first_pos + hm + N_META, full_len=seq_len + N_META)

    def store_h1(s0, y):
        h1_ref[s0:s0 + LN_ROWS, :] = y

    _project_residual_ln(lhs_ref, wout_ref, bout_ref, mlng_ref, mlnb_ref, xe_ref, hm, store_h1)

    if emit_meta:
        @pl.when(i_f == 0)
        def _():
            meta_out_ref[0] = y0_ref[...]


def _resident(shape):
    zeros = (0,) * len(shape)
    return pl.BlockSpec(shape, lambda g: zeros, pipeline_mode=pl.Buffered(1))


def _row(v):
    return v.reshape(1, -1)


def _layer(x, meta, mixer, mixer_params, mixer_scratch, hm, ffn_params, emit_meta, name):
    bsz, seq_len, d_model = x.shape
    hf = HALO_SMALL
    assert seq_len % SEQ_TILE == 0 and SEQ_TILE % PREV_ROWS == 0 and SEQ_TILE % NEXT_ROWS == 0
    assert N_META + hf + hm <= PREV_ROWS and hf + hm <= NEXT_ROWS and N_META % LN_ROWS == 0
    assert meta.shape[1:] == (N_META, d_model) and len(ffn_params) == FFN_PARAMS
    tiles_per_seq = seq_len // SEQ_TILE
    n_tiles = bsz * tiles_per_seq
    prev_per_tile = SEQ_TILE // PREV_ROWS
    next_per_tile = SEQ_TILE // NEXT_ROWS
    last_next = seq_len // NEXT_ROWS - 1
    meta_batched = meta.shape[0] != 1

    def mixer_tile(g):
        return divmod(jnp.minimum(g, n_tiles - 1), tiles_per_seq)

    def ffn_tile(g):
        return divmod(jnp.maximum(g - 1, 0), tiles_per_seq)

    def prev_map(g):
        b, i = mixer_tile(g)
        return (b, jnp.maximum(i * prev_per_tile - 1, 0), 0)

    def tile_map(g):
        b, i = mixer_tile(g)
        return (b, i, 0)

    def next_map(g):
        b, i = mixer_tile(g)
        return (b, jnp.minimum((i + 1) * next_per_tile, last_next), 0)

    def out_map(g):
        b, i = ffn_tile(g)
        return (b, i, 0)

    params = mixer_params + ffn_params
    small = [p for p in params if not isinstance(p, _HbmWeight)]
    big = [p for p in params if isinstance(p, _HbmWeight)]
    hbm_weights = tuple((pos, p.layer, p.chunk_rows) for pos, p in enumerate(params)
                        if isinstance(p, _HbmWeight))
    for p in big:
        assert p.stacked.shape[1] % p.chunk_rows == 0 and p.chunk_rows % 16 == 0
    in_specs = [
        pl.BlockSpec((1, PREV_ROWS, d_model), prev_map),
        pl.BlockSpec((1, SEQ_TILE, d_model), tile_map),
        pl.BlockSpec((1, NEXT_ROWS, d_model), next_map),
        (pl.BlockSpec((1, N_META, d_model), lambda g: (mixer_tile(g)[0], 0, 0))
         if meta_batched else _resident(meta.shape)),
    ] + [_resident(p.shape) for p in small] + [pl.BlockSpec(memory_space=pl.ANY) for _ in big]
    out_specs = [pl.BlockSpec((1, SEQ_TILE, d_model), out_map)]
    out_shape = [jax.ShapeDtypeStruct(x.shape, _F32)]
    if emit_meta:
        out_specs.append(pl.BlockSpec((1, N_META, d_model), lambda g: (ffn_tile(g)[0], 0, 0)))
        out_shape.append(jax.ShapeDtypeStruct((bsz, N_META, d_model), _F32))

    m = SEQ_TILE + N_META
    m2 = m + 2 * hf
    assert m % LN_ROWS == 0 and m2 % LN_ROWS == 0
    d_ff = ffn_params[4].stacked.shape[1]
    assert d_ff % CHUNK == 0
    scratch = [pltpu.VMEM((m2 + 2 * hm, d_model), _F32),
               pltpu.VMEM((m2, d_model), _F32),
               pltpu.VMEM((m2, d_model), _F32),
               pltpu.VMEM((2, m2, CHUNK), _F32),
               pltpu.VMEM((m, d_ff), _BF16),
               pltpu.VMEM((N_META, d_model), _F32)] + mixer_scratch
    scratch += [pltpu.VMEM(p.stacked.shape[1:], _BF16) for p in big]
    outs = pl.pallas_call(
        functools.partial(_layer_kernel, mixer=mixer, n_mixer_params=len(mixer_params), hm=hm,
                          seq_len=seq_len, emit_meta=emit_meta, hbm_weights=hbm_weights),
        grid=(n_tiles + 1,),
        in_specs=in_specs,
        out_specs=out_specs,
        out_shape=out_shape,
        scratch_shapes=scratch,
        compiler_params=pltpu.CompilerParams(
            dimension_semantics=("arbitrary",),
            vmem_limit_bytes=VMEM_LIMIT_BYTES),
        name=name,
    )(x, x, x, meta, *small, *[p.stacked for p in big])
    return (outs[0], outs[1]) if emit_meta else (outs[0], None)


def _hbm_weight(stacked, layer):
    _, rows, cols = stacked.shape
    fits = [r for r in range(16, rows + 1, 16)
            if rows % r == 0 and r * cols * 4 <= WEIGHT_STAGE_BYTES]
    return _HbmWeight(stacked, layer, fits[-1])


def _ffn_param_list(w_up_all, layer, b_up, conv_w, conv_b, w_down_all, b_down, ln_g, ln_b):
    return [_hbm_weight(w_up_all, layer), _row(b_up), conv_w, _row(conv_b),
            _hbm_weight(w_down_all, layer), _row(b_down), _row(ln_g), _row(ln_b)]


def _layer_ab(x, meta, j, w_in_all, b_in, conv_w, conv_b, n_g, n_b, pool_w, pool_scale, w_out_all,
              b_out, ln_g, ln_b, ffn_params, emit_meta):
    k_taps, d_a = conv_w.shape
    d_b = pool_scale.shape[0]
    hm = HALO_AB
    assert k_taps // 2 <= hm and max(POOL_WINDOWS) // 2 <= hm
    assert d_a % CHUNK == 0 and d_b % CHUNK == 0 and (d_a // LANES) % 2 == 0
    m2 = SEQ_TILE + N_META + 2 * HALO_SMALL
    ext = m2 + 2 * hm
    assert m2 % CONV_ROWS == 0 and CONV_ROWS % SUBLANES == 0
    conv_w_rows = jnp.broadcast_to(conv_w[:, None, :], (k_taps, SUBLANES, d_a))
    params = [_hbm_weight(w_in_all, j), _row(b_in), conv_w_rows, _row(conv_b), _row(n_g),
              _row(n_b), pool_w.astype(_BF16), _row(pool_scale), _hbm_weight(w_out_all, j),
              _row(b_out), _row(ln_g), _row(ln_b)]
    scratch = [pltpu.VMEM((ext, d_a), _F32),
               pltpu.VMEM((2, SUBLANES - 1, ext, LANES), _F32),
               pltpu.VMEM((ext, d_b), _F32),
               pltpu.VMEM((m2, d_a), _F32),
               pltpu.VMEM((m2, d_a + d_b), _BF16)]
    return _layer(x, meta, (_mixer_ab_front, _mixer_ab_back), params, scratch, hm, ffn_params,
                  emit_meta, "layer_ab")


def _layer_c(x, meta, j, w_in_all, b_in, conv_w, conv_b, w_out_all, b_out, ln_g, ln_b,
             ffn_params, emit_meta):
    d_c = w_out_all.shape[1]
    hm = HALO_SMALL
    assert d_c % CHUNK == 0
    m2 = SEQ_TILE + N_META + 2 * HALO_SMALL
    params = [_hbm_weight(w_in_all, j), _row(b_in), conv_w, _row(conv_b),
              _hbm_weight(w_out_all, j), _row(b_out), _row(ln_g), _row(ln_b)]
    scratch = [pltpu.VMEM((2, m2 + 2 * hm, CHUNK), _F32),
               pltpu.VMEM((m2, d_c), _BF16)]
    return _layer(x, meta, (_mixer_c_front, _mixer_c_back), params, scratch, hm, ffn_params,
                  emit_meta, "layer_c")


def kernel(x, meta_tokens, w_in_ab, b_in_ab, conv_a_w, conv_a_b, norm_a_g, norm_a_b, pool_w, pool_scale, w_out_ab, b_out_ab, w_in_c, b_in_c, conv_c_w, conv_c_b, w_out_c, b_out_c, mix_ln_g, mix_ln_b, ffn_w_up, ffn_b_up, ffn_conv_w, ffn_conv_b, ffn_w_down, ffn_b_down, ffn_ln_g, ffn_ln_b):
    assert meta_tokens.shape[0] == N_META and mix_ln_g.shape[0] == DEPTH
    h = x
    meta = meta_tokens[None].astype(x.dtype)
    for i in range(DEPTH):
        j = i // 2
        ffn_params = _ffn_param_list(ffn_w_up, i, ffn_b_up[i], ffn_conv_w[i], ffn_conv_b[i],
                                     ffn_w_down, ffn_b_down[i], ffn_ln_g[i], ffn_ln_b[i])
        emit_meta = i + 1 < DEPTH
        if i % 2 == 0:
            h, meta = _layer_ab(h, meta, j, w_in_ab, b_in_ab[j], conv_a_w[j], conv_a_b[j],
                                norm_a_g[j], norm_a_b[j], pool_w[j], pool_scale[j], w_out_ab,
                                b_out_ab[j], mix_ln_g[i], mix_ln_b[i], ffn_params, emit_meta)
        else:
            h, meta = _layer_c(h, meta, j, w_in_c, b_in_c[j], conv_c_w[j], conv_c_b[j],
                               w_out_c, b_out_c[j], mix_ln_g[i], mix_ln_b[i], ffn_params,
                               emit_meta)
    return h
```

```python
import functools
from typing import NamedTuple

import jax
import jax.numpy as jnp
from jax import lax
from jax.experimental import pallas as pl
from jax.experimental.pallas import tpu as pltpu

N_META = 16
POOL_WINDOWS = (2, 4, 8, 16)
LN_EPS = 1e-5
DEPTH = 2
DEEPNORM_ALPHA = (2.0 * DEPTH) ** 0.25

SUBLANES = 8
LANES = 128
SEQ_TILE = 512
PREV_ROWS = 64
NEXT_ROWS = 32
HALO_AB = 16
HALO_SMALL = 8
CHUNK = 256
LN_ROWS = 16
CONV_ROWS = 32
FFN_PARAMS = 8
WEIGHT_STAGE_BYTES = 3 * 512 * 1024
WEIGHT_STAGE_SLOTS = 4
VMEM_LIMIT_BYTES = 60000 * 1024

_BF16 = jnp.bfloat16
_F32 = jnp.float32


def _dot(a, b):
    return jnp.dot(a, b, preferred_element_type=_F32)


def _layer_norm(x, g, b):
    mu = jnp.mean(x, axis=-1, keepdims=True)
    xc = x - mu
    var = jnp.mean(xc * xc, axis=-1, keepdims=True)
    return xc * lax.rsqrt(var + LN_EPS) * g + b


def _row_blocks(n_rows, n_blocks, multiple):
    units = n_rows // multiple
    bounds = [multiple * ((units * k) // n_blocks) for k in range(n_blocks + 1)]
    return list(zip(bounds[:-1], bounds[1:]))


class _HbmWeight(NamedTuple):
    stacked: jax.Array
    layer: int
    chunk_rows: int


class _LayerSlice(NamedTuple):
    stacked: jax.Array
    layer: int


def _load_weight_bf16(w_hbm, layer, chunk_rows, dst_ref):
    rows, cols = dst_ref.shape
    n_chunks = rows // chunk_rows
    n_slots = WEIGHT_STAGE_SLOTS

    def body(stage, sem):
        def copy(c):
            src = w_hbm.at[layer, pl.ds(c * chunk_rows, chunk_rows), :]
            return pltpu.make_async_copy(src, stage.at[c % n_slots], sem.at[c % n_slots])

        for c in range(min(n_slots - 1, n_chunks)):
            copy(c).start()
        for c in range(n_chunks):
            if c + n_slots - 1 < n_chunks:
                copy(c + n_slots - 1).start()
            copy(c).wait()
            dst_ref[c * chunk_rows:(c + 1) * chunk_rows, :] = stage[c % n_slots].astype(_BF16)

    pl.run_scoped(body, pltpu.VMEM((n_slots, chunk_rows, cols), _F32),
                  pltpu.SemaphoreType.DMA((n_slots,)))


def _valid_rows(n_rows, first_pos, seq_len):
    pos = lax.broadcasted_iota(jnp.int32, (n_rows, 1), 0) + first_pos
    return (pos >= -N_META) & (pos < seq_len)


def _project_residual_ln(lhs_ref, w_ref, bias_ref, g_ref, b_ref, res_ref, res_row0, store):
    bias = bias_ref[...]
    g = g_ref[...]
    b = b_ref[...]
    for r0, r1 in _row_blocks(lhs_ref.shape[0], 2, LN_ROWS):
        proj = _dot(lhs_ref[r0:r1, :], w_ref[...])
        for s0 in range(r0, r1, LN_ROWS):
            z = (DEEPNORM_ALPHA * res_ref[res_row0 + s0:res_row0 + s0 + LN_ROWS, :]
                 + (proj[s0 - r0:s0 - r0 + LN_ROWS, :] + bias))
            store(s0, _layer_norm(z, g, b))


def _ffn_body(xe_ref, first_pos, seq_len, params, o_ref, y0_ref, g_ref, a_ref):
    wup_ref, bup_ref, cw_ref, cb_ref, wdn_ref, bdn_ref, lng_ref, lnb_ref = params
    h = HALO_SMALL
    m = a_ref.shape[0]
    d_ff = wdn_ref.shape[0]

    xb = xe_ref[...].astype(_BF16)
    xbm = xe_ref[h:h + m, :].astype(_BF16)
    valid = _valid_rows(m + 2 * h, first_pos, seq_len)

    for j in range(d_ff // CHUNK):
        c0 = j * CHUNK
        gsl = slice(c0, c0 + CHUNK)
        vsl = slice(d_ff + c0, d_ff + c0 + CHUNK)
        g = _dot(xb, wup_ref[:, gsl]) + bup_ref[:, gsl]
        gbuf = g_ref.at[j % 2]
        gbuf[...] = jnp.where(valid, g, 0.0)
        v = _dot(xbm, wup_ref[:, vsl]) + bup_ref[:, vsl]
        gc = (cw_ref[1:2, gsl] * gbuf[h:h + m, :]
              + cw_ref[0:1, gsl] * gbuf[h - 1:h - 1 + m, :]
              + cw_ref[2:3, gsl] * gbuf[h + 1:h + 1 + m, :]
              + cb_ref[:, gsl])
        a_ref[:, gsl] = (jax.nn.silu(gc) * v).astype(_BF16)

    def store(s0, y):
        if s0 < N_META:
            y0_ref[s0:s0 + LN_ROWS, :] = y
        else:
            o_ref[0, s0 - N_META:s0 - N_META + LN_ROWS, :] = y

    _project_residual_ln(a_ref, wdn_ref, bdn_ref, lng_ref, lnb_ref, xe_ref, h, store)


def _mixer_c_front(xe_ref, valid, hm, m2, params, scratch):
    win_ref, bin_ref, cw_ref, cb_ref = params
    s_ref, y_ref = scratch
    d_c = y_ref.shape[1]
    xb = xe_ref[...].astype(_BF16)
    xbm = xe_ref[hm:hm + m2, :].astype(_BF16)

    for j in range(d_c // CHUNK):
        c0 = j * CHUNK
        bsl = slice(c0, c0 + CHUNK)
        csl = slice(d_c + c0, d_c + c0 + CHUNK)
        vsl = slice(2 * d_c + c0, 2 * d_c + c0 + CHUNK)
        cg = _dot(xb, win_ref[:, csl]) + bin_ref[:, csl]
        v = _dot(xb, win_ref[:, vsl]) + bin_ref[:, vsl]
        sbuf = s_ref.at[j % 2]
        sbuf[...] = jnp.where(valid, cg * v, 0.0)
        bg = _dot(xbm, win_ref[:, bsl]) + bin_ref[:, bsl]
        conv = (cw_ref[1:2, bsl] * sbuf[hm:hm + m2, :]
                + cw_ref[0:1, bsl] * sbuf[hm - 1:hm - 1 + m2, :]
                + cw_ref[2:3, bsl] * sbuf[hm + 1:hm + 1 + m2, :]
                + cb_ref[:, bsl])
        y_ref[:, bsl] = (bg * conv).astype(_BF16)


def _mixer_c_back(hm, m2, params, scratch, *, tpos0, full_len):
    del hm, m2, params, tpos0, full_len
    return scratch[1]


def _mixer_ab_front(xe_ref, valid, hm, m2, params, scratch):
    del hm, m2
    win_ref, bin_ref = params[:2]
    a_ref, _, u_ref = scratch[:3]
    d_a = a_ref.shape[1]
    d_b = u_ref.shape[1]
    xb = xe_ref[...].astype(_BF16)

    for c0 in range(0, d_a, CHUNK):
        vsl = slice(c0, c0 + CHUNK)
        gsl = slice(d_a + c0, d_a + c0 + CHUNK)
        a_val = _dot(xb, win_ref[:, vsl]) + bin_ref[:, vsl]
        a_gate = _dot(xb, win_ref[:, gsl]) + bin_ref[:, gsl]
        a_ref[:, vsl] = jnp.where(valid, a_val * jax.nn.sigmoid(a_gate), 0.0)
    for c0 in range(0, d_b, CHUNK):
        usl = slice(2 * d_a + c0, 2 * d_a + c0 + CHUNK)
        u = _dot(xb, win_ref[:, usl]) + bin_ref[:, usl]
        u_ref[:, c0:c0 + CHUNK] = jnp.where(valid, u, 0.0)


def _mixer_ab_back(hm, m2, params, scratch, *, tpos0, full_len):
    _, _, cwb_ref, cb_ref, ng_ref, nb_ref, pw_ref, ps_ref = params
    a_ref, ash_ref, u_ref, c_ref, cat_ref = scratch
    ext = m2 + 2 * hm
    d_a = a_ref.shape[1]
    d_b = u_ref.shape[1]
    pool_c = d_b // len(POOL_WINDOWS)
    k_taps = cwb_ref.shape[0]
    reach = k_taps // 2

    n_sh = ext - SUBLANES
    n_acc = CONV_ROWS // SUBLANES
    for lt in range(d_a // LANES):
        lanes = slice(lt * LANES, (lt + 1) * LANES)
        sh = ash_ref.at[lt % 2]
        for s in range(1, SUBLANES):
            sh[s - 1, 0:n_sh, :] = a_ref[s:s + n_sh, lanes]
        for r0 in range(0, m2, CONV_ROWS):
            accs = [None] * n_acc
            for k in range(k_taps):
                first_row = hm - reach + k
                s, q = first_row % SUBLANES, first_row // SUBLANES
                w = cwb_ref[k, :, lanes]
                for t in range(n_acc):
                    row = r0 + SUBLANES * (q + t)
                    blk = (a_ref[row:row + SUBLANES, lanes] if s == 0
                           else sh[s - 1, row:row + SUBLANES, :])
                    accs[t] = w * blk if accs[t] is None else accs[t] + w * blk
            for t, acc in enumerate(accs):
                c_ref[r0 + SUBLANES * t:r0 + SUBLANES * (t + 1), lanes] = acc + cb_ref[:, lanes]

    ng = ng_ref[...]
    nb = nb_ref[...]
    for r0 in range(0, m2, LN_ROWS):
        an = jax.nn.silu(_layer_norm(c_ref[r0:r0 + LN_ROWS, :], ng, nb))
        cat_ref[r0:r0 + LN_ROWS, 0:d_a] = an.astype(_BF16)

    tpos = jnp.clip(lax.broadcasted_iota(jnp.int32, (m2, 1), 0) + tpos0, 0, full_len - 1)
    for g, w in enumerate(POOL_WINDOWS):
        half = w // 2
        lanes = slice(g * pool_c, (g + 1) * pool_c)
        s = u_ref[hm:hm + m2, lanes]
        for d in range(-half, half):
            if d != 0:
                s = s + u_ref[hm + d:hm + d + m2, lanes]
        cnt = (jnp.minimum(tpos + half, full_len) - jnp.maximum(tpos - half, 0)).astype(_F32)
        p = s / cnt - u_ref[hm:hm + m2, lanes]
        q = _dot(p.astype(_BF16), pw_ref[g]) * ps_ref[:, lanes]
        cat_ref[:, d_a + g * pool_c:d_a + (g + 1) * pool_c] = q.astype(_BF16)
    return cat_ref


def _layer_kernel(prev_ref, x_ref, next_ref, meta_ref, *refs, mixer, n_mixer_params, hm,
                  seq_len, emit_meta, hbm_weights, layer_slices):
    mixer_front, mixer_back = mixer
    n_params = n_mixer_params + FFN_PARAMS
    n_hbm = len(hbm_weights)
    small = list(refs[:n_params - n_hbm])
    for k, layer in enumerate(layer_slices):
        if layer is not None:
            ref = small[k]
            small[k] = ref.at[pl.ds(layer, 1)] if len(ref.shape) == 2 else ref.at[layer]
    w_hbm = refs[n_params - n_hbm:n_params]
    rest = refs[n_params:]
    w_vmem = rest[len(rest) - n_hbm:]
    hbm_pos = [pos for pos, _, _ in hbm_weights]
    params = [w_vmem[hbm_pos.index(p)] if p in hbm_pos else small.pop(0)
              for p in range(n_params)]
    mixer_params = params[:n_mixer_params - 4]
    wout_ref, bout_ref, mlng_ref, mlnb_ref = params[n_mixer_params - 4:n_mixer_params]
    ffn_params = params[n_mixer_params:]
    o_ref = rest[0]
    meta_out_ref = rest[1] if emit_meta else None
    xe_ref, h1_ref, xf_ref, g_ref, af_ref, y0_ref = rest[1 + emit_meta:7 + emit_meta]
    mixer_scratch = rest[7 + emit_meta:len(rest) - n_hbm]

    tl = x_ref.shape[1]
    hf = HALO_SMALL
    tiles_per_seq = seq_len // tl
    n_tiles = pl.num_programs(0) - 1
    g = pl.program_id(0)
    i_m = lax.rem(jnp.minimum(g, n_tiles - 1), jnp.int32(tiles_per_seq))
    i_f = lax.rem(jnp.maximum(g - 1, 0), jnp.int32(tiles_per_seq))
    m = tl + N_META
    m2 = m + 2 * hf
    ext = m2 + 2 * hm
    before = N_META + hf + hm

    @pl.when(g == 0)
    def _():
        h1_ref[...] = jnp.zeros(h1_ref.shape, _F32)
        for k, (_, layer, chunk_rows) in enumerate(hbm_weights):
            _load_weight_bf16(w_hbm[k], layer, chunk_rows, w_vmem[k])

    xf_ref[...] = h1_ref[...]

    xe_ref[0:before, :] = prev_ref[0, PREV_ROWS - before:PREV_ROWS, :]
    xe_ref[hf + hm:before, :] = jnp.where(i_m == 0, meta_ref[0],
                                          prev_ref[0, PREV_ROWS - N_META:PREV_ROWS, :])
    xe_ref[before:before + tl, :] = x_ref[0]
    xe_ref[before + tl:ext, :] = next_ref[0, 0:hf + hm, :]
    first_pos = i_m * tl - before
    valid = _valid_rows(ext, first_pos, seq_len)
    mixer_front(xe_ref, valid, hm, m2, mixer_params, mixer_scratch)

    _ffn_body(xf_ref, i_f * tl - N_META - hf, seq_len, ffn_params, o_ref, y0_ref, g_ref, af_ref)

    lhs_ref = mixer_back(hm, m2, mixer_params, mixer_scratch,
                         tpos0=first_pos + hm + N_META, full_len=seq_len + N_META)

    def store_h1(s0, y):
        h1_ref[s0:s0 + LN_ROWS, :] = y

    _project_residual_ln(lhs_ref, wout_ref, bout_ref, mlng_ref, mlnb_ref, xe_ref, hm, store_h1)

    if emit_meta:
        @pl.when(i_f == 0)
        def _():
            meta_out_ref[0] = y0_ref[...]


def _resident(shape):
    zeros = (0,) * len(shape)
    return pl.BlockSpec(shape, lambda g: zeros, pipeline_mode=pl.Buffered(1))


def _layer(x, meta, mixer, mixer_params, mixer_scratch, hm, ffn_params, emit_meta, name):
    bsz, seq_len, d_model = x.shape
    hf = HALO_SMALL
    assert seq_len % SEQ_TILE == 0 and SEQ_TILE % PREV_ROWS == 0 and SEQ_TILE % NEXT_ROWS == 0
    assert N_META + hf + hm <= PREV_ROWS and hf + hm <= NEXT_ROWS and N_META % LN_ROWS == 0
    assert meta.shape[1:] == (N_META, d_model) and len(ffn_params) == FFN_PARAMS
    tiles_per_seq = seq_len // SEQ_TILE
    n_tiles = bsz * tiles_per_seq
    prev_per_tile = SEQ_TILE // PREV_ROWS
    next_per_tile = SEQ_TILE // NEXT_ROWS
    last_next = seq_len // NEXT_ROWS - 1
    meta_batched = meta.shape[0] != 1

    def mixer_tile(g):
        return divmod(jnp.minimum(g, n_tiles - 1), tiles_per_seq)

    def ffn_tile(g):
        return divmod(jnp.maximum(g - 1, 0), tiles_per_seq)

    def prev_map(g):
        b, i = mixer_tile(g)
        return (b, jnp.maximum(i * prev_per_tile - 1, 0), 0)

    def tile_map(g):
        b, i = mixer_tile(g)
        return (b, i, 0)

    def next_map(g):
        b, i = mixer_tile(g)
        return (b, jnp.minimum((i + 1) * next_per_tile, last_next), 0)

    def out_map(g):
        b, i = ffn_tile(g)
        return (b, i, 0)

    params = mixer_params + ffn_params
    small = [p for p in params if not isinstance(p, _HbmWeight)]
    big = [p for p in params if isinstance(p, _HbmWeight)]
    hbm_weights = tuple((pos, p.layer, p.chunk_rows) for pos, p in enumerate(params)
                        if isinstance(p, _HbmWeight))
    for p in big:
        assert p.stacked.shape[1] % p.chunk_rows == 0 and p.chunk_rows % 16 == 0
    layer_slices = tuple(p.layer if isinstance(p, _LayerSlice) else None for p in small)
    small = [p.stacked if isinstance(p, _LayerSlice) else p for p in small]
    in_specs = [
        pl.BlockSpec((1, PREV_ROWS, d_model), prev_map),
        pl.BlockSpec((1, SEQ_TILE, d_model), tile_map),
        pl.BlockSpec((1, NEXT_ROWS, d_model), next_map),
        (pl.BlockSpec((1, N_META, d_model), lambda g: (mixer_tile(g)[0], 0, 0))
         if meta_batched else _resident(meta.shape)),
    ] + [_resident(p.shape) for p in small] + [pl.BlockSpec(memory_space=pl.ANY) for _ in big]
    out_specs = [pl.BlockSpec((1, SEQ_TILE, d_model), out_map)]
    out_shape = [jax.ShapeDtypeStruct(x.shape, _F32)]
    if emit_meta:
        out_specs.append(pl.BlockSpec((1, N_META, d_model), lambda g: (ffn_tile(g)[0], 0, 0)))
        out_shape.append(jax.ShapeDtypeStruct((bsz, N_META, d_model), _F32))

    m = SEQ_TILE + N_META
    m2 = m + 2 * hf
    assert m % LN_ROWS == 0 and m2 % LN_ROWS == 0
    d_ff = ffn_params[4].stacked.shape[1]
    assert d_ff % CHUNK == 0
    scratch = [pltpu.VMEM((m2 + 2 * hm, d_model), _F32),
               pltpu.VMEM((m2, d_model), _F32),
               pltpu.VMEM((m2, d_model), _F32),
               pltpu.VMEM((2, m2, CHUNK), _F32),
               pltpu.VMEM((m, d_ff), _BF16),
               pltpu.VMEM((N_META, d_model), _F32)] + mixer_scratch
    scratch += [pltpu.VMEM(p.stacked.shape[1:], _BF16) for p in big]
    outs = pl.pallas_call(
        functools.partial(_layer_kernel, mixer=mixer, n_mixer_params=len(mixer_params), hm=hm,
                          seq_len=seq_len, emit_meta=emit_meta, hbm_weights=hbm_weights,
                          layer_slices=layer_slices),
        grid=(n_tiles + 1,),
        in_specs=in_specs,
        out_specs=out_specs,
        out_shape=out_shape,
        scratch_shapes=scratch,
        compiler_params=pltpu.CompilerParams(
            dimension_semantics=("arbitrary",),
            vmem_limit_bytes=VMEM_LIMIT_BYTES),
        name=name,
    )(x, x, x, meta, *small, *[p.stacked for p in big])
    return (outs[0], outs[1]) if emit_meta else (outs[0], None)


def _hbm_weight(stacked, layer):
    _, rows, cols = stacked.shape
    fits = [r for r in range(16, rows + 1, 16)
            if rows % r == 0 and r * cols * 4 <= WEIGHT_STAGE_BYTES]
    return _HbmWeight(stacked, layer, fits[-1])


def _ffn_param_list(layer, w_up, b_up, conv_w, conv_b, w_down, b_down, ln_g, ln_b):
    pick = functools.partial(_LayerSlice, layer=layer)
    return [_hbm_weight(w_up, layer), pick(b_up), pick(conv_w), pick(conv_b),
            _hbm_weight(w_down, layer), pick(b_down), pick(ln_g), pick(ln_b)]


def _layer_ab(x, meta, j, w_in, b_in, conv_w, conv_b, n_g, n_b, pool_w, pool_scale, w_out, b_out,
              ln, ln_g, ln_b, ffn_params, emit_meta):
    _, k_taps, d_a = conv_w.shape
    d_b = pool_scale.shape[1]
    pick = functools.partial(_LayerSlice, layer=j)
    hm = HALO_AB
    assert k_taps // 2 <= hm and max(POOL_WINDOWS) // 2 <= hm
    assert d_a % CHUNK == 0 and d_b % CHUNK == 0 and (d_a // LANES) % 2 == 0
    m2 = SEQ_TILE + N_META + 2 * HALO_SMALL
    ext = m2 + 2 * hm
    assert m2 % CONV_ROWS == 0 and CONV_ROWS % SUBLANES == 0
    conv_w_rows = jnp.broadcast_to(conv_w[j][:, None, :], (k_taps, SUBLANES, d_a))
    params = [_hbm_weight(w_in, j), pick(b_in), conv_w_rows, pick(conv_b), pick(n_g),
              pick(n_b), pool_w[j].astype(_BF16), pick(pool_scale), _hbm_weight(w_out, j),
              pick(b_out), _LayerSlice(ln_g, ln), _LayerSlice(ln_b, ln)]
    scratch = [pltpu.VMEM((ext, d_a), _F32),
               pltpu.VMEM((2, SUBLANES - 1, ext, LANES), _F32),
               pltpu.VMEM((ext, d_b), _F32),
               pltpu.VMEM((m2, d_a), _F32),
               pltpu.VMEM((m2, d_a + d_b), _BF16)]
    return _layer(x, meta, (_mixer_ab_front, _mixer_ab_back), params, scratch, hm, ffn_params,
                  emit_meta, "layer_ab")


def _layer_c(x, meta, j, w_in, b_in, conv_w, conv_b, w_out, b_out, ln, ln_g, ln_b, ffn_params,
             emit_meta):
    d_c = w_out.shape[1]
    pick = functools.partial(_LayerSlice, layer=j)
    hm = HALO_SMALL
    assert d_c % CHUNK == 0
    m2 = SEQ_TILE + N_META + 2 * HALO_SMALL
    params = [_hbm_weight(w_in, j), pick(b_in), pick(conv_w), pick(conv_b),
              _hbm_weight(w_out, j), pick(b_out), _LayerSlice(ln_g, ln), _LayerSlice(ln_b, ln)]
    scratch = [pltpu.VMEM((2, m2 + 2 * hm, CHUNK), _F32),
               pltpu.VMEM((m2, d_c), _BF16)]
    return _layer(x, meta, (_mixer_c_front, _mixer_c_back), params, scratch, hm, ffn_params,
                  emit_meta, "layer_c")


def kernel(x, meta_tokens, w_in_ab, b_in_ab, conv_a_w, conv_a_b, norm_a_g, norm_a_b, pool_w, pool_scale, w_out_ab, b_out_ab, w_in_c, b_in_c, conv_c_w, conv_c_b, w_out_c, b_out_c, mix_ln_g, mix_ln_b, ffn_w_up, ffn_b_up, ffn_conv_w, ffn_conv_b, ffn_w_down, ffn_b_down, ffn_ln_g, ffn_ln_b):
    assert meta_tokens.shape[0] == N_META and mix_ln_g.shape[0] == DEPTH
    h = x
    meta = meta_tokens[None].astype(x.dtype)
    for i in range(DEPTH):
        j = i // 2
        ffn_params = _ffn_param_list(i, ffn_w_up, ffn_b_up, ffn_conv_w, ffn_conv_b, ffn_w_down,
                                     ffn_b_down, ffn_ln_g, ffn_ln_b)
        emit_meta = i + 1 < DEPTH
        if i % 2 == 0:
            h, meta = _layer_ab(h, meta, j, w_in_ab, b_in_ab, conv_a_w, conv_a_b, norm_a_g,
                                norm_a_b, pool_w, pool_scale, w_out_ab, b_out_ab, i, mix_ln_g,
                                mix_ln_b, ffn_params, emit_meta)
        else:
            h, meta = _layer_c(h, meta, j, w_in_c, b_in_c, conv_c_w, conv_c_b, w_out_c, b_out_c,
                               i, mix_ln_g, mix_ln_b, ffn_params, emit_meta)
    return h
```

```python
import functools
from typing import NamedTuple

import jax
import jax.numpy as jnp
from jax import lax
from jax.experimental import pallas as pl
from jax.experimental.pallas import tpu as pltpu

N_META = 16
POOL_WINDOWS = (2, 4, 8, 16)
LN_EPS = 1e-5
DEPTH = 2
DEEPNORM_ALPHA = (2.0 * DEPTH) ** 0.25

SUBLANES = 8
LANES = 128
SEQ_TILE = 512
PREV_ROWS = 64
NEXT_ROWS = 32
HALO_AB = 16
HALO_SMALL = 8
CHUNK = 256
LN_ROWS = 16
CONV_ROWS = 32
FFN_PARAMS = 8
WEIGHT_STAGE_BYTES = 3 * 512 * 1024
WEIGHT_STAGE_SLOTS = 4
VMEM_LIMIT_BYTES = 60000 * 1024

_BF16 = jnp.bfloat16
_F32 = jnp.float32


def _dot(a, b):
    return jnp.dot(a, b, preferred_element_type=_F32)


def _layer_norm(x, g, b):
    mu = jnp.mean(x, axis=-1, keepdims=True)
    xc = x - mu
    var = jnp.mean(xc * xc, axis=-1, keepdims=True)
    return xc * lax.rsqrt(var + LN_EPS) * g + b


def _row_blocks(n_rows, n_blocks, multiple):
    units = n_rows // multiple
    bounds = [multiple * ((units * k) // n_blocks) for k in range(n_blocks + 1)]
    return list(zip(bounds[:-1], bounds[1:]))


class _HbmWeight(NamedTuple):
    stacked: jax.Array
    layer: int
    chunk_rows: int


class _LayerSlice(NamedTuple):
    stacked: jax.Array
    layer: int


def _load_weight_bf16(w_hbm, layer, chunk_rows, dst_ref):
    rows, cols = dst_ref.shape
    n_chunks = rows // chunk_rows
    n_slots = WEIGHT_STAGE_SLOTS

    def body(stage, sem):
        def copy(c):
            src = w_hbm.at[layer, pl.ds(c * chunk_rows, chunk_rows), :]
            return pltpu.make_async_copy(src, stage.at[c % n_slots], sem.at[c % n_slots])

        for c in range(min(n_slots - 1, n_chunks)):
            copy(c).start()
        for c in range(n_chunks):
            if c + n_slots - 1 < n_chunks:
                copy(c + n_slots - 1).start()
            copy(c).wait()
            dst_ref[c * chunk_rows:(c + 1) * chunk_rows, :] = stage[c % n_slots].astype(_BF16)

    pl.run_scoped(body, pltpu.VMEM((n_slots, chunk_rows, cols), _F32),
                  pltpu.SemaphoreType.DMA((n_slots,)))


def _valid_rows(n_rows, first_pos, seq_len):
    pos = lax.broadcasted_iota(jnp.int32, (n_rows, 1), 0) + first_pos
    return (pos >= -N_META) & (pos < seq_len)


def _project_residual_ln(lhs_ref, w_ref, bias_ref, g_ref, b_ref, res_ref, res_row0, store):
    bias = bias_ref[...]
    g = g_ref[...]
    b = b_ref[...]
    for r0, r1 in _row_blocks(lhs_ref.shape[0], 2, LN_ROWS):
        proj = _dot(lhs_ref[r0:r1, :], w_ref[...])
        for s0 in range(r0, r1, LN_ROWS):
            z = (DEEPNORM_ALPHA * res_ref[res_row0 + s0:res_row0 + s0 + LN_ROWS, :]
                 + (proj[s0 - r0:s0 - r0 + LN_ROWS, :] + bias))
            store(s0, _layer_norm(z, g, b))


def _ffn_body(xe_ref, first_pos, seq_len, params, o_ref, y0_ref, g_ref, a_ref):
    wup_ref, bup_ref, cw_ref, cb_ref, wdn_ref, bdn_ref, lng_ref, lnb_ref = params
    h = HALO_SMALL
    m = a_ref.shape[0]
    d_ff = wdn_ref.shape[0]

    xb = xe_ref[...].astype(_BF16)
    xbm = xe_ref[h:h + m, :].astype(_BF16)
    valid = _valid_rows(m + 2 * h, first_pos, seq_len)

    for j in range(d_ff // CHUNK):
        c0 = j * CHUNK
        gsl = slice(c0, c0 + CHUNK)
        vsl = slice(d_ff + c0, d_ff + c0 + CHUNK)
        g = _dot(xb, wup_ref[:, gsl]) + bup_ref[:, gsl]
        gbuf = g_ref.at[j % 2]
        gbuf[...] = jnp.where(valid, g, 0.0)
        v = _dot(xbm, wup_ref[:, vsl]) + bup_ref[:, vsl]
        gc = (cw_ref[1:2, gsl] * gbuf[h:h + m, :]
              + cw_ref[0:1, gsl] * gbuf[h - 1:h - 1 + m, :]
              + cw_ref[2:3, gsl] * gbuf[h + 1:h + 1 + m, :]
              + cb_ref[:, gsl])
        a_ref[:, gsl] = (jax.nn.silu(gc) * v).astype(_BF16)

    def store(s0, y):
        if s0 < N_META:
            y0_ref[s0:s0 + LN_ROWS, :] = y
        else:
            o_ref[0, s0 - N_META:s0 - N_META + LN_ROWS, :] = y

    _project_residual_ln(a_ref, wdn_ref, bdn_ref, lng_ref, lnb_ref, xe_ref, h, store)


def _mixer_c_front(xe_ref, valid, hm, m2, params, scratch):
    win_ref, bin_ref, cw_ref, cb_ref = params
    s_ref, y_ref = scratch
    d_c = y_ref.shape[1]
    xb = xe_ref[...].astype(_BF16)
    xbm = xe_ref[hm:hm + m2, :].astype(_BF16)

    for j in range(d_c // CHUNK):
        c0 = j * CHUNK
        bsl = slice(c0, c0 + CHUNK)
        csl = slice(d_c + c0, d_c + c0 + CHUNK)
        vsl = slice(2 * d_c + c0, 2 * d_c + c0 + CHUNK)
        cg = _dot(xb, win_ref[:, csl]) + bin_ref[:, csl]
        v = _dot(xb, win_ref[:, vsl]) + bin_ref[:, vsl]
        sbuf = s_ref.at[j % 2]
        sbuf[...] = jnp.where(valid, cg * v, 0.0)
        bg = _dot(xbm, win_ref[:, bsl]) + bin_ref[:, bsl]
        conv = (cw_ref[1:2, bsl] * sbuf[hm:hm + m2, :]
                + cw_ref[0:1, bsl] * sbuf[hm - 1:hm - 1 + m2, :]
                + cw_ref[2:3, bsl] * sbuf[hm + 1:hm + 1 + m2, :]
                + cb_ref[:, bsl])
        y_ref[:, bsl] = (bg * conv).astype(_BF16)


def _mixer_c_back(hm, m2, params, scratch, *, tpos0, full_len):
    del hm, m2, params, tpos0, full_len
    return scratch[1]


def _mixer_ab_front(xe_ref, valid, hm, m2, params, scratch):
    del hm, m2
    win_ref, bin_ref = params[:2]
    a_ref, _, u_ref = scratch[:3]
    d_a = a_ref.shape[1]
    d_b = u_ref.shape[1]
    xb = xe_ref[...].astype(_BF16)

    for c0 in range(0, d_a, CHUNK):
        vsl = slice(c0, c0 + CHUNK)
        gsl = slice(d_a + c0, d_a + c0 + CHUNK)
        a_val = _dot(xb, win_ref[:, vsl]) + bin_ref[:, vsl]
        a_gate = _dot(xb, win_ref[:, gsl]) + bin_ref[:, gsl]
        a_ref[:, vsl] = jnp.where(valid, a_val * jax.nn.sigmoid(a_gate), 0.0)
    for c0 in range(0, d_b, CHUNK):
        usl = slice(2 * d_a + c0, 2 * d_a + c0 + CHUNK)
        u = _dot(xb, win_ref[:, usl]) + bin_ref[:, usl]
        u_ref[:, c0:c0 + CHUNK] = jnp.where(valid, u, 0.0)


def _mixer_ab_prologue(params, scratch):
    cw_ref, cwb_ref = params[2], scratch[5]
    for k in range(cwb_ref.shape[0]):
        cwb_ref[k] = jnp.broadcast_to(cw_ref[k:k + 1, :], cwb_ref.shape[1:])


def _mixer_ab_back(hm, m2, params, scratch, *, tpos0, full_len):
    _, _, _, cb_ref, ng_ref, nb_ref, pw_ref, ps_ref = params
    a_ref, ash_ref, u_ref, c_ref, cat_ref, cwb_ref = scratch
    ext = m2 + 2 * hm
    d_a = a_ref.shape[1]
    d_b = u_ref.shape[1]
    pool_c = d_b // len(POOL_WINDOWS)
    k_taps = cwb_ref.shape[0]
    reach = k_taps // 2

    n_sh = ext - SUBLANES
    n_acc = CONV_ROWS // SUBLANES
    for lt in range(d_a // LANES):
        lanes = slice(lt * LANES, (lt + 1) * LANES)
        sh = ash_ref.at[lt % 2]
        for s in range(1, SUBLANES):
            sh[s - 1, 0:n_sh, :] = a_ref[s:s + n_sh, lanes]
        for r0 in range(0, m2, CONV_ROWS):
            accs = [None] * n_acc
            for k in range(k_taps):
                first_row = hm - reach + k
                s, q = first_row % SUBLANES, first_row // SUBLANES
                w = cwb_ref[k, :, lanes]
                for t in range(n_acc):
                    row = r0 + SUBLANES * (q + t)
                    blk = (a_ref[row:row + SUBLANES, lanes] if s == 0
                           else sh[s - 1, row:row + SUBLANES, :])
                    accs[t] = w * blk if accs[t] is None else accs[t] + w * blk
            for t, acc in enumerate(accs):
                c_ref[r0 + SUBLANES * t:r0 + SUBLANES * (t + 1), lanes] = acc + cb_ref[:, lanes]

    ng = ng_ref[...]
    nb = nb_ref[...]
    for r0 in range(0, m2, LN_ROWS):
        an = jax.nn.silu(_layer_norm(c_ref[r0:r0 + LN_ROWS, :], ng, nb))
        cat_ref[r0:r0 + LN_ROWS, 0:d_a] = an.astype(_BF16)

    tpos = jnp.clip(lax.broadcasted_iota(jnp.int32, (m2, 1), 0) + tpos0, 0, full_len - 1)
    for g, w in enumerate(POOL_WINDOWS):
        half = w // 2
        lanes = slice(g * pool_c, (g + 1) * pool_c)
        s = u_ref[hm:hm + m2, lanes]
        for d in range(-half, half):
            if d != 0:
                s = s + u_ref[hm + d:hm + d + m2, lanes]
        cnt = (jnp.minimum(tpos + half, full_len) - jnp.maximum(tpos - half, 0)).astype(_F32)
        p = s / cnt - u_ref[hm:hm + m2, lanes]
        q = _dot(p.astype(_BF16), pw_ref[g].astype(_BF16)) * ps_ref[:, lanes]
        cat_ref[:, d_a + g * pool_c:d_a + (g + 1) * pool_c] = q.astype(_BF16)
    return cat_ref


def _layer_kernel(prev_ref, x_ref, next_ref, meta_ref, *refs, mixer, n_mixer_params, hm,
                  seq_len, emit_meta, hbm_weights, layer_slices):
    mixer_front, mixer_back, mixer_prologue = mixer
    n_params = n_mixer_params + FFN_PARAMS
    n_hbm = len(hbm_weights)
    small = list(refs[:n_params - n_hbm])
    for k, layer in enumerate(layer_slices):
        if layer is not None:
            ref = small[k]
            small[k] = ref.at[pl.ds(layer, 1)] if len(ref.shape) == 2 else ref.at[layer]
    w_hbm = refs[n_params - n_hbm:n_params]
    rest = refs[n_params:]
    w_vmem = rest[len(rest) - n_hbm:]
    hbm_pos = [pos for pos, _, _ in hbm_weights]
    params = [w_vmem[hbm_pos.index(p)] if p in hbm_pos else small.pop(0)
              for p in range(n_params)]
    mixer_params = params[:n_mixer_params - 4]
    wout_ref, bout_ref, mlng_ref, mlnb_ref = params[n_mixer_params - 4:n_mixer_params]
    ffn_params = params[n_mixer_params:]
    o_ref = rest[0]
    meta_out_ref = rest[1] if emit_meta else None
    xe_ref, h1_ref, xf_ref, g_ref, af_ref, y0_ref = rest[1 + emit_meta:7 + emit_meta]
    mixer_scratch = rest[7 + emit_meta:len(rest) - n_hbm]

    tl = x_ref.shape[1]
    hf = HALO_SMALL
    tiles_per_seq = seq_len // tl
    n_tiles = pl.num_programs(0) - 1
    g = pl.program_id(0)
    i_m = lax.rem(jnp.minimum(g, n_tiles - 1), jnp.int32(tiles_per_seq))
    i_f = lax.rem(jnp.maximum(g - 1, 0), jnp.int32(tiles_per_seq))
    m = tl + N_META
    m2 = m + 2 * hf
    ext = m2 + 2 * hm
    before = N_META + hf + hm

    @pl.when(g == 0)
    def _():
        h1_ref[...] = jnp.zeros(h1_ref.shape, _F32)
        for k, (_, layer, chunk_rows) in enumerate(hbm_weights):
            _load_weight_bf16(w_hbm[k], layer, chunk_rows, w_vmem[k])
        if mixer_prologue is not None:
            mixer_prologue(mixer_params, mixer_scratch)

    xf_ref[...] = h1_ref[...]

    xe_ref[0:before, :] = prev_ref[0, PREV_ROWS - before:PREV_ROWS, :]
    xe_ref[hf + hm:before, :] = jnp.where(i_m == 0, meta_ref[0],
                                          prev_ref[0, PREV_ROWS - N_META:PREV_ROWS, :])
    xe_ref[before:before + tl, :] = x_ref[0]
    xe_ref[before + tl:ext, :] = next_ref[0, 0:hf + hm, :]
    first_pos = i_m * tl - before
    valid = _valid_rows(ext, first_pos, seq_len)
    mixer_front(xe_ref, valid, hm, m2, mixer_params, mixer_scratch)

    _ffn_body(xf_ref, i_f * tl - N_META - hf, seq_len, ffn_params, o_ref, y0_ref, g_ref, af_ref)

    lhs_ref = mixer_back(hm, m2, mixer_params, mixer_scratch,
                         tpos0=first_pos + hm + N_META, full_len=seq_len + N_META)

    def store_h1(s0, y):
        h1_ref[s0:s0 + LN_ROWS, :] = y

    _project_residual_ln(lhs_ref, wout_ref, bout_ref, mlng_ref, mlnb_ref, xe_ref, hm, store_h1)

    if emit_meta:
        @pl.when(i_f == 0)
        def _():
            meta_out_ref[0] = y0_ref[...]


def _resident(shape):
    zeros = (0,) * len(shape)
    return pl.BlockSpec(shape, lambda g: zeros, pipeline_mode=pl.Buffered(1))


def _layer(x, meta, mixer, mixer_params, mixer_scratch, hm, ffn_params, emit_meta, name):
    bsz, seq_len, d_model = x.shape
    hf = HALO_SMALL
    assert seq_len % SEQ_TILE == 0 and SEQ_TILE % PREV_ROWS == 0 and SEQ_TILE % NEXT_ROWS == 0
    assert N_META + hf + hm <= PREV_ROWS and hf + hm <= NEXT_ROWS and N_META % LN_ROWS == 0
    assert meta.shape[1:] == (N_META, d_model) and len(ffn_params) == FFN_PARAMS
    tiles_per_seq = seq_len // SEQ_TILE
    n_tiles = bsz * tiles_per_seq
    prev_per_tile = SEQ_TILE // PREV_ROWS
    next_per_tile = SEQ_TILE // NEXT_ROWS
    last_next = seq_len // NEXT_ROWS - 1
    meta_batched = meta.shape[0] != 1

    def mixer_tile(g):
        return divmod(jnp.minimum(g, n_tiles - 1), tiles_per_seq)

    def ffn_tile(g):
        return divmod(jnp.maximum(g - 1, 0), tiles_per_seq)

    def prev_map(g):
        b, i = mixer_tile(g)
        return (b, jnp.maximum(i * prev_per_tile - 1, 0), 0)

    def tile_map(g):
        b, i = mixer_tile(g)
        return (b, i, 0)

    def next_map(g):
        b, i = mixer_tile(g)
        return (b, jnp.minimum((i + 1) * next_per_tile, last_next), 0)

    def out_map(g):
        b, i = ffn_tile(g)
        return (b, i, 0)

    params = mixer_params + ffn_params
    small = [p for p in params if not isinstance(p, _HbmWeight)]
    big = [p for p in params if isinstance(p, _HbmWeight)]
    hbm_weights = tuple((pos, p.layer, p.chunk_rows) for pos, p in enumerate(params)
                        if isinstance(p, _HbmWeight))
    for p in big:
        assert p.stacked.shape[1] % p.chunk_rows == 0 and p.chunk_rows % 16 == 0
    layer_slices = tuple(p.layer if isinstance(p, _LayerSlice) else None for p in small)
    small = [p.stacked if isinstance(p, _LayerSlice) else p for p in small]
    in_specs = [
        pl.BlockSpec((1, PREV_ROWS, d_model), prev_map),
        pl.BlockSpec((1, SEQ_TILE, d_model), tile_map),
        pl.BlockSpec((1, NEXT_ROWS, d_model), next_map),
        (pl.BlockSpec((1, N_META, d_model), lambda g: (mixer_tile(g)[0], 0, 0))
         if meta_batched else _resident(meta.shape)),
    ] + [_resident(p.shape) for p in small] + [pl.BlockSpec(memory_space=pl.ANY) for _ in big]
    out_specs = [pl.BlockSpec((1, SEQ_TILE, d_model), out_map)]
    out_shape = [jax.ShapeDtypeStruct(x.shape, _F32)]
    if emit_meta:
        out_specs.append(pl.BlockSpec((1, N_META, d_model), lambda g: (ffn_tile(g)[0], 0, 0)))
        out_shape.append(jax.ShapeDtypeStruct((bsz, N_META, d_model), _F32))

    m = SEQ_TILE + N_META
    m2 = m + 2 * hf
    assert m % LN_ROWS == 0 and m2 % LN_ROWS == 0
    d_ff = ffn_params[4].stacked.shape[1]
    assert d_ff % CHUNK == 0
    scratch = [pltpu.VMEM((m2 + 2 * hm, d_model), _F32),
               pltpu.VMEM((m2, d_model), _F32),
               pltpu.VMEM((m2, d_model), _F32),
               pltpu.VMEM((2, m2, CHUNK), _F32),
               pltpu.VMEM((m, d_ff), _BF16),
               pltpu.VMEM((N_META, d_model), _F32)] + mixer_scratch
    scratch += [pltpu.VMEM(p.stacked.shape[1:], _BF16) for p in big]
    outs = pl.pallas_call(
        functools.partial(_layer_kernel, mixer=mixer, n_mixer_params=len(mixer_params), hm=hm,
                          seq_len=seq_len, emit_meta=emit_meta, hbm_weights=hbm_weights,
                          layer_slices=layer_slices),
        grid=(n_tiles + 1,),
        in_specs=in_specs,
        out_specs=out_specs,
        out_shape=out_shape,
        scratch_shapes=scratch,
        compiler_params=pltpu.CompilerParams(
            dimension_semantics=("arbitrary",),
            vmem_limit_bytes=VMEM_LIMIT_BYTES),
        name=name,
    )(x, x, x, meta, *small, *[p.stacked for p in big])
    return (outs[0], outs[1]) if emit_meta else (outs[0], None)


def _hbm_weight(stacked, layer):
    _, rows, cols = stacked.shape
    fits = [r for r in range(16, rows + 1, 16)
            if rows % r == 0 and r * cols * 4 <= WEIGHT_STAGE_BYTES]
    return _HbmWeight(stacked, layer, fits[-1])


def _ffn_param_list(layer, w_up, b_up, conv_w, conv_b, w_down, b_down, ln_g, ln_b):
    pick = functools.partial(_LayerSlice, layer=layer)
    return [_hbm_weight(w_up, layer), pick(b_up), pick(conv_w), pick(conv_b),
            _hbm_weight(w_down, layer), pick(b_down), pick(ln_g), pick(ln_b)]


def _layer_ab(x, meta, j, w_in, b_in, conv_w, conv_b, n_g, n_b, pool_w, pool_scale, w_out, b_out,
              ln, ln_g, ln_b, ffn_params, emit_meta):
    _, k_taps, d_a = conv_w.shape
    d_b = pool_scale.shape[1]
    pick = functools.partial(_LayerSlice, layer=j)
    hm = HALO_AB
    assert k_taps // 2 <= hm and max(POOL_WINDOWS) // 2 <= hm
    assert d_a % CHUNK == 0 and d_b % CHUNK == 0 and (d_a // LANES) % 2 == 0
    m2 = SEQ_TILE + N_META + 2 * HALO_SMALL
    ext = m2 + 2 * hm
    assert m2 % CONV_ROWS == 0 and CONV_ROWS % SUBLANES == 0
    params = [_hbm_weight(w_in, j), pick(b_in), pick(conv_w), pick(conv_b), pick(n_g),
              pick(n_b), pick(pool_w), pick(pool_scale), _hbm_weight(w_out, j),
              pick(b_out), _LayerSlice(ln_g, ln), _LayerSlice(ln_b, ln)]
    scratch = [pltpu.VMEM((ext, d_a), _F32),
               pltpu.VMEM((2, SUBLANES - 1, ext, LANES), _F32),
               pltpu.VMEM((ext, d_b), _F32),
               pltpu.VMEM((m2, d_a), _F32),
               pltpu.VMEM((m2, d_a + d_b), _BF16),
               pltpu.VMEM((k_taps, SUBLANES, d_a), _F32)]
    return _layer(x, meta, (_mixer_ab_front, _mixer_ab_back, _mixer_ab_prologue), params, scratch, hm, ffn_params,
                  emit_meta, "layer_ab")


def _layer_c(x, meta, j, w_in, b_in, conv_w, conv_b, w_out, b_out, ln, ln_g, ln_b, ffn_params,
             emit_meta):
    d_c = w_out.shape[1]
    pick = functools.partial(_LayerSlice, layer=j)
    hm = HALO_SMALL
    assert d_c % CHUNK == 0
    m2 = SEQ_TILE + N_META + 2 * HALO_SMALL
    params = [_hbm_weight(w_in, j), pick(b_in), pick(conv_w), pick(conv_b),
              _hbm_weight(w_out, j), pick(b_out), _LayerSlice(ln_g, ln), _LayerSlice(ln_b, ln)]
    scratch = [pltpu.VMEM((2, m2 + 2 * hm, CHUNK), _F32),
               pltpu.VMEM((m2, d_c), _BF16)]
    return _layer(x, meta, (_mixer_c_front, _mixer_c_back, None), params, scratch, hm, ffn_params,
                  emit_meta, "layer_c")


def kernel(x, meta_tokens, w_in_ab, b_in_ab, conv_a_w, conv_a_b, norm_a_g, norm_a_b, pool_w, pool_scale, w_out_ab, b_out_ab, w_in_c, b_in_c, conv_c_w, conv_c_b, w_out_c, b_out_c, mix_ln_g, mix_ln_b, ffn_w_up, ffn_b_up, ffn_conv_w, ffn_conv_b, ffn_w_down, ffn_b_down, ffn_ln_g, ffn_ln_b):
    assert meta_tokens.shape[0] == N_META and mix_ln_g.shape[0] == DEPTH
    h = x
    meta = meta_tokens[None].astype(x.dtype)
    for i in range(DEPTH):
        j = i // 2
        ffn_params = _ffn_param_list(i, ffn_w_up, ffn_b_up, ffn_conv_w, ffn_conv_b, ffn_w_down,
                                     ffn_b_down, ffn_ln_g, ffn_ln_b)
        emit_meta = i + 1 < DEPTH
        if i % 2 == 0:
            h, meta = _layer_ab(h, meta, j, w_in_ab, b_in_ab, conv_a_w, conv_a_b, norm_a_g,
                                norm_a_b, pool_w, pool_scale, w_out_ab, b_out_ab, i, mix_ln_g,
                                mix_ln_b, ffn_params, emit_meta)
        else:
            h, meta = _layer_c(h, meta, j, w_in_c, b_in_c, conv_c_w, conv_c_b, w_out_c, b_out_c,
                               i, mix_ln_g, mix_ln_b, ffn_params, emit_meta)
    return h
```

```python
import functools
from typing import NamedTuple

import jax
import jax.numpy as jnp
from jax import lax
from jax.experimental import pallas as pl
from jax.experimental.pallas import tpu as pltpu

N_META = 16
POOL_WINDOWS = (2, 4, 8, 16)
LN_EPS = 1e-5
DEPTH = 2
DEEPNORM_ALPHA = (2.0 * DEPTH) ** 0.25

SUBLANES = 8
LANES = 128
SEQ_TILE = 512
PREV_ROWS = 64
NEXT_ROWS = 32
HALO_AB = 16
HALO_SMALL = 8
CHUNK = 256
LN_ROWS = 16
CONV_ROWS = 32
FFN_PARAMS = 8
WEIGHT_STAGE_BYTES = 3 * 512 * 1024
WEIGHT_STAGE_SLOTS = 5
VMEM_LIMIT_BYTES = 60000 * 1024

_BF16 = jnp.bfloat16
_F32 = jnp.float32


def _dot(a, b):
    return jnp.dot(a, b, preferred_element_type=_F32)


def _layer_norm(x, g, b):
    mu = jnp.mean(x, axis=-1, keepdims=True)
    xc = x - mu
    var = jnp.mean(xc * xc, axis=-1, keepdims=True)
    return xc * lax.rsqrt(var + LN_EPS) * g + b


def _row_blocks(n_rows, n_blocks, multiple):
    units = n_rows // multiple
    bounds = [multiple * ((units * k) // n_blocks) for k in range(n_blocks + 1)]
    return list(zip(bounds[:-1], bounds[1:]))


class _HbmWeight(NamedTuple):
    stacked: jax.Array
    layer: int
    chunk_rows: int


class _LayerSlice(NamedTuple):
    stacked: jax.Array
    layer: int


def _load_weight_bf16(w_hbm, layer, chunk_rows, dst_ref):
    rows, cols = dst_ref.shape
    n_chunks = rows // chunk_rows
    n_slots = WEIGHT_STAGE_SLOTS

    def body(stage, sem):
        def copy(c):
            src = w_hbm.at[layer, pl.ds(c * chunk_rows, chunk_rows), :]
            return pltpu.make_async_copy(src, stage.at[c % n_slots], sem.at[c % n_slots])

        for c in range(min(n_slots - 1, n_chunks)):
            copy(c).start()
        for c in range(n_chunks):
            if c + n_slots - 1 < n_chunks:
                copy(c + n_slots - 1).start()
            copy(c).wait()
            dst_ref[c * chunk_rows:(c + 1) * chunk_rows, :] = stage[c % n_slots].astype(_BF16)

    pl.run_scoped(body, pltpu.VMEM((n_slots, chunk_rows, cols), _F32),
                  pltpu.SemaphoreType.DMA((n_slots,)))


def _valid_rows(n_rows, first_pos, seq_len):
    pos = lax.broadcasted_iota(jnp.int32, (n_rows, 1), 0) + first_pos
    return (pos >= -N_META) & (pos < seq_len)


def _project_residual_ln(lhs_ref, w_ref, bias_ref, g_ref, b_ref, res_ref, res_row0, store):
    bias = bias_ref[...]
    g = g_ref[...]
    b = b_ref[...]
    for r0, r1 in _row_blocks(lhs_ref.shape[0], 2, LN_ROWS):
        proj = _dot(lhs_ref[r0:r1, :], w_ref[...])
        for s0 in range(r0, r1, LN_ROWS):
            z = (DEEPNORM_ALPHA * res_ref[res_row0 + s0:res_row0 + s0 + LN_ROWS, :]
                 + (proj[s0 - r0:s0 - r0 + LN_ROWS, :] + bias))
            store(s0, _layer_norm(z, g, b))


def _ffn_body(xe_ref, first_pos, seq_len, params, o_ref, y0_ref, g_ref, a_ref):
    wup_ref, bup_ref, cw_ref, cb_ref, wdn_ref, bdn_ref, lng_ref, lnb_ref = params
    h = HALO_SMALL
    m = a_ref.shape[0]
    d_ff = wdn_ref.shape[0]

    xb = xe_ref[...].astype(_BF16)
    xbm = xe_ref[h:h + m, :].astype(_BF16)
    valid = _valid_rows(m + 2 * h, first_pos, seq_len)

    for j in range(d_ff // CHUNK):
        c0 = j * CHUNK
        gsl = slice(c0, c0 + CHUNK)
        vsl = slice(d_ff + c0, d_ff + c0 + CHUNK)
        g = _dot(xb, wup_ref[:, gsl]) + bup_ref[:, gsl]
        gbuf = g_ref.at[j % 2]
        gbuf[...] = jnp.where(valid, g, 0.0)
        v = _dot(xbm, wup_ref[:, vsl]) + bup_ref[:, vsl]
        gc = (cw_ref[1:2, gsl] * gbuf[h:h + m, :]
              + cw_ref[0:1, gsl] * gbuf[h - 1:h - 1 + m, :]
              + cw_ref[2:3, gsl] * gbuf[h + 1:h + 1 + m, :]
              + cb_ref[:, gsl])
        a_ref[:, gsl] = (jax.nn.silu(gc) * v).astype(_BF16)

    def store(s0, y):
        if s0 < N_META:
            y0_ref[s0:s0 + LN_ROWS, :] = y
        else:
            o_ref[0, s0 - N_META:s0 - N_META + LN_ROWS, :] = y

    _project_residual_ln(a_ref, wdn_ref, bdn_ref, lng_ref, lnb_ref, xe_ref, h, store)


def _mixer_c_front(xe_ref, valid, hm, m2, params, scratch):
    win_ref, bin_ref, cw_ref, cb_ref = params
    s_ref, y_ref = scratch
    d_c = y_ref.shape[1]
    xb = xe_ref[...].astype(_BF16)
    xbm = xe_ref[hm:hm + m2, :].astype(_BF16)

    for j in range(d_c // CHUNK):
        c0 = j * CHUNK
        bsl = slice(c0, c0 + CHUNK)
        csl = slice(d_c + c0, d_c + c0 + CHUNK)
        vsl = slice(2 * d_c + c0, 2 * d_c + c0 + CHUNK)
        cg = _dot(xb, win_ref[:, csl]) + bin_ref[:, csl]
        v = _dot(xb, win_ref[:, vsl]) + bin_ref[:, vsl]
        sbuf = s_ref.at[j % 2]
        sbuf[...] = jnp.where(valid, cg * v, 0.0)
        bg = _dot(xbm, win_ref[:, bsl]) + bin_ref[:, bsl]
        conv = (cw_ref[1:2, bsl] * sbuf[hm:hm + m2, :]
                + cw_ref[0:1, bsl] * sbuf[hm - 1:hm - 1 + m2, :]
                + cw_ref[2:3, bsl] * sbuf[hm + 1:hm + 1 + m2, :]
                + cb_ref[:, bsl])
        y_ref[:, bsl] = (bg * conv).astype(_BF16)


def _mixer_c_back(hm, m2, params, scratch, *, tpos0, full_len):
    del hm, m2, params, tpos0, full_len
    return scratch[1]


def _mixer_ab_front(xe_ref, valid, hm, m2, params, scratch):
    del hm, m2
    win_ref, bin_ref = params[:2]
    a_ref, _, u_ref = scratch[:3]
    d_a = a_ref.shape[1]
    d_b = u_ref.shape[1]
    xb = xe_ref[...].astype(_BF16)

    for c0 in range(0, d_a, CHUNK):
        vsl = slice(c0, c0 + CHUNK)
        gsl = slice(d_a + c0, d_a + c0 + CHUNK)
        a_val = _dot(xb, win_ref[:, vsl]) + bin_ref[:, vsl]
        a_gate = _dot(xb, win_ref[:, gsl]) + bin_ref[:, gsl]
        a_ref[:, vsl] = jnp.where(valid, a_val * jax.nn.sigmoid(a_gate), 0.0)
    for c0 in range(0, d_b, CHUNK):
        usl = slice(2 * d_a + c0, 2 * d_a + c0 + CHUNK)
        u = _dot(xb, win_ref[:, usl]) + bin_ref[:, usl]
        u_ref[:, c0:c0 + CHUNK] = jnp.where(valid, u, 0.0)


def _mixer_ab_prologue(params, scratch):
    cw_ref, cwb_ref = params[2], scratch[5]
    for k in range(cwb_ref.shape[0]):
        cwb_ref[k] = jnp.broadcast_to(cw_ref[k:k + 1, :], cwb_ref.shape[1:])


def _mixer_ab_back(hm, m2, params, scratch, *, tpos0, full_len):
    _, _, _, cb_ref, ng_ref, nb_ref, pw_ref, ps_ref = params
    a_ref, ash_ref, u_ref, c_ref, cat_ref, cwb_ref = scratch
    ext = m2 + 2 * hm
    d_a = a_ref.shape[1]
    d_b = u_ref.shape[1]
    pool_c = d_b // len(POOL_WINDOWS)
    k_taps = cwb_ref.shape[0]
    reach = k_taps // 2

    n_sh = ext - SUBLANES
    n_acc = CONV_ROWS // SUBLANES
    for lt in range(d_a // LANES):
        lanes = slice(lt * LANES, (lt + 1) * LANES)
        sh = ash_ref.at[lt % 2]
        for s in range(1, SUBLANES):
            sh[s - 1, 0:n_sh, :] = a_ref[s:s + n_sh, lanes]
        for r0 in range(0, m2, CONV_ROWS):
            accs = [None] * n_acc
            for k in range(k_taps):
                first_row = hm - reach + k
                s, q = first_row % SUBLANES, first_row // SUBLANES
                w = cwb_ref[k, :, lanes]
                for t in range(n_acc):
                    row = r0 + SUBLANES * (q + t)
                    blk = (a_ref[row:row + SUBLANES, lanes] if s == 0
                           else sh[s - 1, row:row + SUBLANES, :])
                    accs[t] = w * blk if accs[t] is None else accs[t] + w * blk
            for t, acc in enumerate(accs):
                c_ref[r0 + SUBLANES * t:r0 + SUBLANES * (t + 1), lanes] = acc + cb_ref[:, lanes]

    ng = ng_ref[...]
    nb = nb_ref[...]
    for r0 in range(0, m2, LN_ROWS):
        an = jax.nn.silu(_layer_norm(c_ref[r0:r0 + LN_ROWS, :], ng, nb))
        cat_ref[r0:r0 + LN_ROWS, 0:d_a] = an.astype(_BF16)

    tpos = jnp.clip(lax.broadcasted_iota(jnp.int32, (m2, 1), 0) + tpos0, 0, full_len - 1)
    for g, w in enumerate(POOL_WINDOWS):
        half = w // 2
        lanes = slice(g * pool_c, (g + 1) * pool_c)
        s = u_ref[hm:hm + m2, lanes]
        for d in range(-half, half):
            if d != 0:
                s = s + u_ref[hm + d:hm + d + m2, lanes]
        cnt = (jnp.minimum(tpos + half, full_len) - jnp.maximum(tpos - half, 0)).astype(_F32)
        p = s / cnt - u_ref[hm:hm + m2, lanes]
        q = _dot(p.astype(_BF16), pw_ref[g].astype(_BF16)) * ps_ref[:, lanes]
        cat_ref[:, d_a + g * pool_c:d_a + (g + 1) * pool_c] = q.astype(_BF16)
    return cat_ref


def _layer_kernel(prev_ref, x_ref, next_ref, meta_ref, *refs, mixer, n_mixer_params, hm,
                  seq_len, emit_meta, hbm_weights, layer_slices):
    mixer_front, mixer_back, mixer_prologue = mixer
    n_params = n_mixer_params + FFN_PARAMS
    n_hbm = len(hbm_weights)
    small = list(refs[:n_params - n_hbm])
    for k, layer in enumerate(layer_slices):
        if layer is not None:
            ref = small[k]
            small[k] = ref.at[pl.ds(layer, 1)] if len(ref.shape) == 2 else ref.at[layer]
    w_hbm = refs[n_params - n_hbm:n_params]
    rest = refs[n_params:]
    w_vmem = rest[len(rest) - n_hbm:]
    hbm_pos = [pos for pos, _, _ in hbm_weights]
    params = [w_vmem[hbm_pos.index(p)] if p in hbm_pos else small.pop(0)
              for p in range(n_params)]
    mixer_params = params[:n_mixer_params - 4]
    wout_ref, bout_ref, mlng_ref, mlnb_ref = params[n_mixer_params - 4:n_mixer_params]
    ffn_params = params[n_mixer_params:]
    o_ref = rest[0]
    meta_out_ref = rest[1] if emit_meta else None
    xe_ref, h1_ref, xf_ref, g_ref, af_ref, y0_ref = rest[1 + emit_meta:7 + emit_meta]
    mixer_scratch = rest[7 + emit_meta:len(rest) - n_hbm]

    tl = x_ref.shape[1]
    hf = HALO_SMALL
    tiles_per_seq = seq_len // tl
    n_tiles = pl.num_programs(0) - 1
    g = pl.program_id(0)
    i_m = lax.rem(jnp.minimum(g, n_tiles - 1), jnp.int32(tiles_per_seq))
    i_f = lax.rem(jnp.maximum(g - 1, 0), jnp.int32(tiles_per_seq))
    m = tl + N_META
    m2 = m + 2 * hf
    ext = m2 + 2 * hm
    before = N_META + hf + hm

    @pl.when(g == 0)
    def _():
        h1_ref[...] = jnp.zeros(h1_ref.shape, _F32)
        for k, (_, layer, chunk_rows) in enumerate(hbm_weights):
            _load_weight_bf16(w_hbm[k], layer, chunk_rows, w_vmem[k])
        if mixer_prologue is not None:
            mixer_prologue(mixer_params, mixer_scratch)

    xf_ref[...] = h1_ref[...]

    xe_ref[0:before, :] = prev_ref[0, PREV_ROWS - before:PREV_ROWS, :]
    xe_ref[hf + hm:before, :] = jnp.where(i_m == 0, meta_ref[0],
                                          prev_ref[0, PREV_ROWS - N_META:PREV_ROWS, :])
    xe_ref[before:before + tl, :] = x_ref[0]
    xe_ref[before + tl:ext, :] = next_ref[0, 0:hf + hm, :]
    first_pos = i_m * tl - before
    valid = _valid_rows(ext, first_pos, seq_len)
    mixer_front(xe_ref, valid, hm, m2, mixer_params, mixer_scratch)

    _ffn_body(xf_ref, i_f * tl - N_META - hf, seq_len, ffn_params, o_ref, y0_ref, g_ref, af_ref)

    lhs_ref = mixer_back(hm, m2, mixer_params, mixer_scratch,
                         tpos0=first_pos + hm + N_META, full_len=seq_len + N_META)

    def store_h1(s0, y):
        h1_ref[s0:s0 + LN_ROWS, :] = y

    _project_residual_ln(lhs_ref, wout_ref, bout_ref, mlng_ref, mlnb_ref, xe_ref, hm, store_h1)

    if emit_meta:
        @pl.when(i_f == 0)
        def _():
            meta_out_ref[0] = y0_ref[...]


def _resident(shape):
    zeros = (0,) * len(shape)
    return pl.BlockSpec(shape, lambda g: zeros, pipeline_mode=pl.Buffered(1))


def _layer(x, meta, mixer, mixer_params, mixer_scratch, hm, ffn_params, emit_meta, name):
    bsz, seq_len, d_model = x.shape
    hf = HALO_SMALL
    assert seq_len % SEQ_TILE == 0 and SEQ_TILE % PREV_ROWS == 0 and SEQ_TILE % NEXT_ROWS == 0
    assert N_META + hf + hm <= PREV_ROWS and hf + hm <= NEXT_ROWS and N_META % LN_ROWS == 0
    assert meta.shape[1:] == (N_META, d_model) and len(ffn_params) == FFN_PARAMS
    tiles_per_seq = seq_len // SEQ_TILE
    n_tiles = bsz * tiles_per_seq
    prev_per_tile = SEQ_TILE // PREV_ROWS
    next_per_tile = SEQ_TILE // NEXT_ROWS
    last_next = seq_len // NEXT_ROWS - 1
    meta_batched = meta.shape[0] != 1

    def mixer_tile(g):
        return divmod(jnp.minimum(g, n_tiles - 1), tiles_per_seq)

    def ffn_tile(g):
        return divmod(jnp.maximum(g - 1, 0), tiles_per_seq)

    def prev_map(g):
        b, i = mixer_tile(g)
        return (b, jnp.maximum(i * prev_per_tile - 1, 0), 0)

    def tile_map(g):
        b, i = mixer_tile(g)
        return (b, i, 0)

    def next_map(g):
        b, i = mixer_tile(g)
        return (b, jnp.minimum((i + 1) * next_per_tile, last_next), 0)

    def out_map(g):
        b, i = ffn_tile(g)
        return (b, i, 0)

    params = mixer_params + ffn_params
    small = [p for p in params if not isinstance(p, _HbmWeight)]
    big = [p for p in params if isinstance(p, _HbmWeight)]
    hbm_weights = tuple((pos, p.layer, p.chunk_rows) for pos, p in enumerate(params)
                        if isinstance(p, _HbmWeight))
    for p in big:
        assert p.stacked.shape[1] % p.chunk_rows == 0 and p.chunk_rows % 16 == 0
    layer_slices = tuple(p.layer if isinstance(p, _LayerSlice) else None for p in small)
    small = [p.stacked if isinstance(p, _LayerSlice) else p for p in small]
    in_specs = [
        pl.BlockSpec((1, PREV_ROWS, d_model), prev_map),
        pl.BlockSpec((1, SEQ_TILE, d_model), tile_map),
        pl.BlockSpec((1, NEXT_ROWS, d_model), next_map),
        (pl.BlockSpec((1, N_META, d_model), lambda g: (mixer_tile(g)[0], 0, 0))
         if meta_batched else _resident(meta.shape)),
    ] + [_resident(p.shape) for p in small] + [pl.BlockSpec(memory_space=pl.ANY) for _ in big]
    out_specs = [pl.BlockSpec((1, SEQ_TILE, d_model), out_map)]
    out_shape = [jax.ShapeDtypeStruct(x.shape, _F32)]
    if emit_meta:
        out_specs.append(pl.BlockSpec((1, N_META, d_model), lambda g: (ffn_tile(g)[0], 0, 0)))
        out_shape.append(jax.ShapeDtypeStruct((bsz, N_META, d_model), _F32))

    m = SEQ_TILE + N_META
    m2 = m + 2 * hf
    assert m % LN_ROWS == 0 and m2 % LN_ROWS == 0
    d_ff = ffn_params[4].stacked.shape[1]
    assert d_ff % CHUNK == 0
    scratch = [pltpu.VMEM((m2 + 2 * hm, d_model), _F32),
               pltpu.VMEM((m2, d_model), _F32),
               pltpu.VMEM((m2, d_model), _F32),
               pltpu.VMEM((2, m2, CHUNK), _F32),
               pltpu.VMEM((m, d_ff), _BF16),
               pltpu.VMEM((N_META, d_model), _F32)] + mixer_scratch
    scratch += [pltpu.VMEM(p.stacked.shape[1:], _BF16) for p in big]
    outs = pl.pallas_call(
        functools.partial(_layer_kernel, mixer=mixer, n_mixer_params=len(mixer_params), hm=hm,
                          seq_len=seq_len, emit_meta=emit_meta, hbm_weights=hbm_weights,
                          layer_slices=layer_slices),
        grid=(n_tiles + 1,),
        in_specs=in_specs,
        out_specs=out_specs,
        out_shape=out_shape,
        scratch_shapes=scratch,
        compiler_params=pltpu.CompilerParams(
            dimension_semantics=("arbitrary",),
            vmem_limit_bytes=VMEM_LIMIT_BYTES),
        name=name,
    )(x, x, x, meta, *small, *[p.stacked for p in big])
    return (outs[0], outs[1]) if emit_meta else (outs[0], None)


def _hbm_weight(stacked, layer):
    _, rows, cols = stacked.shape
    fits = [r for r in range(16, rows + 1, 16)
            if rows % r == 0 and r * cols * 4 <= WEIGHT_STAGE_BYTES]
    return _HbmWeight(stacked, layer, fits[-1])


def _ffn_param_list(layer, w_up, b_up, conv_w, conv_b, w_down, b_down, ln_g, ln_b):
    pick = functools.partial(_LayerSlice, layer=layer)
    return [_hbm_weight(w_up, layer), pick(b_up), pick(conv_w), pick(conv_b),
            _hbm_weight(w_down, layer), pick(b_down), pick(ln_g), pick(ln_b)]


def _layer_ab(x, meta, j, w_in, b_in, conv_w, conv_b, n_g, n_b, pool_w, pool_scale, w_out, b_out,
              ln, ln_g, ln_b, ffn_params, emit_meta):
    _, k_taps, d_a = conv_w.shape
    d_b = pool_scale.shape[1]
    pick = functools.partial(_LayerSlice, layer=j)
    hm = HALO_AB
    assert k_taps // 2 <= hm and max(POOL_WINDOWS) // 2 <= hm
    assert d_a % CHUNK == 0 and d_b % CHUNK == 0 and (d_a // LANES) % 2 == 0
    m2 = SEQ_TILE + N_META + 2 * HALO_SMALL
    ext = m2 + 2 * hm
    assert m2 % CONV_ROWS == 0 and CONV_ROWS % SUBLANES == 0
    params = [_hbm_weight(w_in, j), pick(b_in), pick(conv_w), pick(conv_b), pick(n_g),
              pick(n_b), pick(pool_w), pick(pool_scale), _hbm_weight(w_out, j),
              pick(b_out), _LayerSlice(ln_g, ln), _LayerSlice(ln_b, ln)]
    scratch = [pltpu.VMEM((ext, d_a), _F32),
               pltpu.VMEM((2, SUBLANES - 1, ext, LANES), _F32),
               pltpu.VMEM((ext, d_b), _F32),
               pltpu.VMEM((m2, d_a), _F32),
               pltpu.VMEM((m2, d_a + d_b), _BF16),
               pltpu.VMEM((k_taps, SUBLANES, d_a), _F32)]
    return _layer(x, meta, (_mixer_ab_front, _mixer_ab_back, _mixer_ab_prologue), params, scratch, hm, ffn_params,
                  emit_meta, "layer_ab")


def _layer_c(x, meta, j, w_in, b_in, conv_w, conv_b, w_out, b_out, ln, ln_g, ln_b, ffn_params,
             emit_meta):
    d_c = w_out.shape[1]
    pick = functools.partial(_LayerSlice, layer=j)
    hm = HALO_SMALL
    assert d_c % CHUNK == 0
    m2 = SEQ_TILE + N_META + 2 * HALO_SMALL
    params = [_hbm_weight(w_in, j), pick(b_in), pick(conv_w), pick(conv_b),
              _hbm_weight(w_out, j), pick(b_out), _LayerSlice(ln_g, ln), _LayerSlice(ln_b, ln)]
    scratch = [pltpu.VMEM((2, m2 + 2 * hm, CHUNK), _F32),
               pltpu.VMEM((m2, d_c), _BF16)]
    return _layer(x, meta, (_mixer_c_front, _mixer_c_back, None), params, scratch, hm, ffn_params,
                  emit_meta, "layer_c")


def kernel(x, meta_tokens, w_in_ab, b_in_ab, conv_a_w, conv_a_b, norm_a_g, norm_a_b, pool_w, pool_scale, w_out_ab, b_out_ab, w_in_c, b_in_c, conv_c_w, conv_c_b, w_out_c, b_out_c, mix_ln_g, mix_ln_b, ffn_w_up, ffn_b_up, ffn_conv_w, ffn_conv_b, ffn_w_down, ffn_b_down, ffn_ln_g, ffn_ln_b):
    assert meta_tokens.shape[0] == N_META and mix_ln_g.shape[0] == DEPTH
    h = x
    meta = meta_tokens[None].astype(x.dtype)
    for i in range(DEPTH):
        j = i // 2
        ffn_params = _ffn_param_list(i, ffn_w_up, ffn_b_up, ffn_conv_w, ffn_conv_b, ffn_w_down,
                                     ffn_b_down, ffn_ln_g, ffn_ln_b)
        emit_meta = i + 1 < DEPTH
        if i % 2 == 0:
            h, meta = _layer_ab(h, meta, j, w_in_ab, b_in_ab, conv_a_w, conv_a_b, norm_a_g,
                                norm_a_b, pool_w, pool_scale, w_out_ab, b_out_ab, i, mix_ln_g,
                                mix_ln_b, ffn_params, emit_meta)
        else:
            h, meta = _layer_c(h, meta, j, w_in_c, b_in_c, conv_c_w, conv_c_b, w_out_c, b_out_c,
                               i, mix_ln_g, mix_ln_b, ffn_params, emit_meta)
    return h
```
